```python
import jax, jax.numpy as jnp
from jax import lax
import numpy as np

D_MODEL = 1024
BATCH = 1
SEQ = 16384
DEPTH = 4

D_MIX = D_MODEL
D_RNN = D_MIX // 2
RNN_HEADS = 8
RNN_BLOCK = D_RNN // RNN_HEADS
CONV_WIDTH = 4
CONV_PAD = (2, 1)
LRU_C = 8.0
GLA_HEADS = 4
GLA_DV_TOTAL = D_MIX - D_RNN
GLA_DK_TOTAL = GLA_DV_TOTAL // 2
GLA_DV = GLA_DV_TOTAL // GLA_HEADS
GLA_DK = GLA_DK_TOTAL // GLA_HEADS
GLA_RANK = 16
GLA_TAU = 16.0
GLA_CHUNK = 64
D_FF = ((8 * D_MODEL + 2) // 3 + 255) // 256 * 256
D_IN = 2 * D_RNN + 2 * GLA_DK_TOTAL + 2 * GLA_DV_TOTAL + 2 * GLA_RANK
RMS_EPS = 1e-6

kernel_name = 'hymba_rglru_gla_encoder'


def rms_norm(x, gain):
    xf = x.astype(jnp.float32)
    y = xf * lax.rsqrt(jnp.mean(xf * xf, axis=-1, keepdims=True) + RMS_EPS)
    return (y * gain.astype(jnp.float32)).astype(x.dtype)


def centred_depthwise_conv(x, w, b):
    y = lax.conv_general_dilated(
        x, w[:, None, :], window_strides=(1,), padding=[CONV_PAD],
        dimension_numbers=('NWC', 'WIO', 'NWC'), feature_group_count=x.shape[-1])
    return y + b


def linear_scan(a, u, reverse):
    def combine(c1, c2):
        a1, b1 = c1
        a2, b2 = c2
        return a1 * a2, a2 * b1 + b2
    _, h = lax.associative_scan(combine, (a, u), axis=1, reverse=reverse)
    return h


def rg_lru(x, w_a, b_a, w_x, b_x, lam, reverse):
    B, S, _ = x.shape
    xf = x.astype(jnp.float32)
    xh = xf.reshape(B, S, RNN_HEADS, RNN_BLOCK)
    r = jax.nn.sigmoid(jnp.einsum('bshi,hij->bshj', xh, w_a.astype(jnp.float32)).reshape(B, S, D_RNN) + b_a)
    i = jax.nn.sigmoid(jnp.einsum('bshi,hij->bshj', xh, w_x.astype(jnp.float32)).reshape(B, S, D_RNN) + b_x)
    log_a = -LRU_C * r * jax.nn.softplus(-lam.astype(jnp.float32))
    a = jnp.exp(log_a)
    u = xf * i * jnp.sqrt(-jnp.expm1(2.0 * log_a))
    return linear_scan(a, u, reverse)


def gla_chunked(q, k, v, log_a):
    B, S, H, DK = q.shape
    DV = v.shape[-1]
    N = S // GLA_CHUNK
    q = q.astype(jnp.float32).reshape(B, N, GLA_CHUNK, H, DK)
    k = k.astype(jnp.float32).reshape(B, N, GLA_CHUNK, H, DK)
    v = v.astype(jnp.float32).reshape(B, N, GLA_CHUNK, H, DV)
    b = jnp.cumsum(log_a.astype(jnp.float32).reshape(B, N, GLA_CHUNK, H, DK), axis=2)
    b_last = b[:, :, -1]
    q_e = q * jnp.exp(b)
    k_e = k * jnp.exp(-b)
    scores = jnp.einsum('bnihd,bnjhd->bnhij', q_e, k_e)
    mask = jnp.tril(jnp.ones((GLA_CHUNK, GLA_CHUNK), dtype=bool))
    scores = jnp.where(mask, scores, 0.0)
    o_intra = jnp.einsum('bnhij,bnjhv->bnihv', scores, v)
    k_dec = k * jnp.exp(b_last[:, :, None] - b)
    u = jnp.einsum('bnjhd,bnjhv->bnhdv', k_dec, v)
    decay = jnp.exp(b_last)

    def step(state, inp):
        d, u_n = inp
        return d[..., None] * state + u_n, state

    _, s_prev = lax.scan(step, jnp.zeros((B, H, DK, DV), jnp.float32),
                         (jnp.moveaxis(decay, 1, 0), jnp.moveaxis(u, 1, 0)))
    s_prev = jnp.moveaxis(s_prev, 0, 1)
    o_inter = jnp.einsum('bnihd,bnhdv->bnihv', q_e, s_prev)
    return (o_intra + o_inter).reshape(B, S, H, DV)


def hybrid_mixer(h, w_in, conv_w, conv_b, lru_w_a, lru_b_a, lru_w_x, lru_b_x, lru_lambda,
                 rnn_out_norm, gla_w_gate, gla_b_gate, gla_out_norm, w_out):
    B, S, _ = h.shape
    proj = jnp.einsum('bsd,de->bse', h, w_in)
    p1 = D_RNN
    p2 = p1 + D_RNN
    p3 = p2 + GLA_DK_TOTAL
    p4 = p3 + GLA_DK_TOTAL
    p5 = p4 + GLA_DV_TOTAL
    p6 = p5 + GLA_DV_TOTAL
    p7 = p6 + GLA_RANK
    x_r, gate_r, q, k, v, g, lr_f, lr_b = jnp.split(proj, [p1, p2, p3, p4, p5, p6, p7], axis=-1)

    xc = centred_depthwise_conv(x_r, conv_w, conv_b)
    h_f = rg_lru(xc, lru_w_a[0], lru_b_a[0], lru_w_x[0], lru_b_x[0], lru_lambda[0], False)
    h_b = rg_lru(xc, lru_w_a[1], lru_b_a[1], lru_w_x[1], lru_b_x[1], lru_lambda[1], True)
    y_rnn = (h_f + h_b) * jax.nn.gelu(gate_r.astype(jnp.float32))
    y_rnn = rms_norm(y_rnn, rnn_out_norm).astype(h.dtype)

    qh = q.reshape(B, S, GLA_HEADS, GLA_DK) * (GLA_DK ** -0.5)
    kh = k.reshape(B, S, GLA_HEADS, GLA_DK)
    vh = v.reshape(B, S, GLA_HEADS, GLA_DV)
    la_f = jax.nn.log_sigmoid((jnp.einsum('bsr,re->bse', lr_f, gla_w_gate[0]) + gla_b_gate[0]).astype(jnp.float32)) / GLA_TAU
    la_b = jax.nn.log_sigmoid((jnp.einsum('bsr,re->bse', lr_b, gla_w_gate[1]) + gla_b_gate[1]).astype(jnp.float32)) / GLA_TAU
    la_f = la_f.reshape(B, S, GLA_HEADS, GLA_DK)
    la_b = la_b.reshape(B, S, GLA_HEADS, GLA_DK)
    o_f = gla_chunked(qh, kh, vh, la_f)
    o_b = jnp.flip(gla_chunked(jnp.flip(qh, 1), jnp.flip(kh, 1), jnp.flip(vh, 1), jnp.flip(la_b, 1)), 1)
    o = rms_norm(o_f + o_b, gla_out_norm)
    o = o * jax.nn.silu(g.astype(jnp.float32).reshape(B, S, GLA_HEADS, GLA_DV))
    y_gla = o.reshape(B, S, GLA_DV_TOTAL).astype(h.dtype)

    y = jnp.concatenate([y_rnn, y_gla], axis=-1)
    return jnp.einsum('bse,ed->bsd', y, w_out)


def swiglu(h, w_gate, w_up, w_down):
    a = jnp.einsum('bsd,df->bsf', h, w_gate)
    u = jnp.einsum('bsd,df->bsf', h, w_up)
    return jnp.einsum('bsf,fd->bsd', jax.nn.silu(a) * u, w_down)


def setup_inputs(seed: int = 0) -> dict:
    key = jax.random.key(seed)
    ks = jax.random.split(key, 24)
    L = DEPTH

    def nrm(k, shape, scale):
        return jax.random.normal(k, shape, jnp.float32) * scale

    def gain(k, shape):
        return 1.0 + nrm(k, shape, 0.02)

    u = jax.random.uniform(ks[10], (L, 2, D_RNN), jnp.float32, minval=0.9, maxval=0.999)
    a = u ** (1.0 / LRU_C)
    lam = jnp.log(a) - jnp.log1p(-a)
    return {
        'x': nrm(ks[0], (BATCH, SEQ, D_MODEL), 1.0),
        'mix_norm_pre': gain(ks[1], (L, D_MODEL)),
        'mix_norm_post': gain(ks[2], (L, D_MODEL)),
        'w_in': nrm(ks[3], (L, D_MODEL, D_IN), D_MODEL ** -0.5),
        'conv_w': nrm(ks[4], (L, CONV_WIDTH, D_RNN), CONV_WIDTH ** -0.5),
        'conv_b': nrm(ks[5], (L, D_RNN), 0.01),
        'lru_w_a': nrm(ks[6], (L, 2, RNN_HEADS, RNN_BLOCK, RNN_BLOCK), RNN_BLOCK ** -0.5),
        'lru_b_a': nrm(ks[7], (L, 2, D_RNN), 0.01),
        'lru_w_x': nrm(ks[8], (L, 2, RNN_HEADS, RNN_BLOCK, RNN_BLOCK), RNN_BLOCK ** -0.5),
        'lru_b_x': nrm(ks[9], (L, 2, D_RNN), 0.01),
        'lru_lambda': lam,
        'rnn_out_norm': gain(ks[11], (L, D_RNN)),
        'gla_w_gate': nrm(ks[12], (L, 2, GLA_RANK, GLA_DK_TOTAL), GLA_RANK ** -0.5),
        'gla_b_gate': nrm(ks[13], (L, 2, GLA_DK_TOTAL), 0.1),
        'gla_out_norm': gain(ks[14], (L, GLA_DV)),
        'w_out': nrm(ks[15], (L, D_MIX, D_MODEL), D_MIX ** -0.5),
        'ffn_norm_pre': gain(ks[16], (L, D_MODEL)),
        'ffn_norm_post': gain(ks[17], (L, D_MODEL)),
        'w_ffn_gate': nrm(ks[18], (L, D_MODEL, D_FF), D_MODEL ** -0.5),
        'w_ffn_up': nrm(ks[19], (L, D_MODEL, D_FF), D_MODEL ** -0.5),
        'w_ffn_down': nrm(ks[20], (L, D_FF, D_MODEL), D_FF ** -0.5),
    }


def reference(x, mix_norm_pre, mix_norm_post, w_in, conv_w, conv_b, lru_w_a, lru_b_a,
              lru_w_x, lru_b_x, lru_lambda, rnn_out_norm, gla_w_gate, gla_b_gate,
              gla_out_norm, w_out, ffn_norm_pre, ffn_norm_post, w_ffn_gate, w_ffn_up,
              w_ffn_down):
    for l in range(DEPTH):
        h = rms_norm(x, mix_norm_pre[l])
        m = hybrid_mixer(h, w_in[l], conv_w[l], conv_b[l], lru_w_a[l], lru_b_a[l],
                         lru_w_x[l], lru_b_x[l], lru_lambda[l], rnn_out_norm[l],
                         gla_w_gate[l], gla_b_gate[l], gla_out_norm[l], w_out[l])
        x = x + rms_norm(m, mix_norm_post[l])
        h = rms_norm(x, ffn_norm_pre[l])
        f = swiglu(h, w_ffn_gate[l], w_ffn_up[l], w_ffn_down[l])
        x = x + rms_norm(f, ffn_norm_post[l])
    return x
```

```python
import functools

import jax
import jax.numpy as jnp
from jax import lax
from jax.experimental import pallas as pl
from jax.experimental.pallas import tpu as pltpu

F32 = jnp.float32
BF16 = jnp.bfloat16

D_MODEL = 1024
D_RNN = 512
RNN_HEADS = 8
RNN_BLOCK = D_RNN // RNN_HEADS
CONV_WIDTH = 4
LRU_C = 8.0
GLA_HEADS = 4
GLA_DV_TOTAL = 512
GLA_DK_TOTAL = 256
GLA_DV = GLA_DV_TOTAL // GLA_HEADS
GLA_DK = GLA_DK_TOTAL // GLA_HEADS
GLA_RANK = 16
GLA_TAU = 16.0
GLA_CHUNK = 64
RMS_EPS = 1e-6

SUBLANES = 8
LANES = 128
HALO_ROWS = SUBLANES
VMEM_LIMIT_BYTES = 56 * 1024 * 1024

TILE_PROJ = 512
TILE_LRU = 512
TILE_GLA = 256
TILE_OUT = 256

_P_XR = 0
_P_GR = _P_XR + D_RNN
_P_Q = _P_GR + D_RNN
_P_K = _P_Q + GLA_DK_TOTAL
_P_V = _P_K + GLA_DK_TOTAL
_P_G = _P_V + GLA_DV_TOTAL
_P_END = _P_G + GLA_DV_TOTAL


def _rms_norm(x, gain):
    ms = jnp.mean(x * x, axis=-1, keepdims=True)
    return x * lax.rsqrt(ms + RMS_EPS) * gain


def _sigmoid(x):
    return 1.0 / (1.0 + jnp.exp(-x))


def _softplus(x):
    return jnp.maximum(x, 0.0) + jnp.log1p(jnp.exp(-jnp.abs(x)))


def _gelu_tanh(x):
    c = 0.7978845608028654
    return 0.5 * x * (1.0 + jnp.tanh(c * (x + 0.044715 * (x * x * x))))


def _dot(a, b):
    return jnp.dot(a, b, preferred_element_type=F32)


def _inproj_kernel(x_ref, gain_ref, w_ref, wlr_ref, wg_ref, bg_ref,
                   xr_ref, gate_ref, q_ref, k_ref, v_ref, g_ref, la_ref):
    h = _rms_norm(x_ref[...], gain_ref[...]).astype(BF16)
    xr_ref[...] = _dot(h, w_ref[:, _P_XR:_P_GR])
    gate_ref[...] = _gelu_tanh(_dot(h, w_ref[:, _P_GR:_P_Q]))
    q_ref[...] = _dot(h, w_ref[:, _P_Q:_P_K]) * (GLA_DK ** -0.5)
    k_ref[...] = _dot(h, w_ref[:, _P_K:_P_V])
    v_ref[...] = _dot(h, w_ref[:, _P_V:_P_G]).astype(BF16)
    g = _dot(h, w_ref[:, _P_G:_P_END])
    g_ref[...] = g * _sigmoid(g)
    lr = _dot(h, wlr_ref[...]).astype(BF16)
    z = _dot(lr, wg_ref[...]) + bg_ref[...]
    la_ref[...] = -_softplus(-z) * (1.0 / GLA_TAU)


def _inproj(x, gain, w_main, w_lr, w_gate, b_gate):
    S = x.shape[0]
    T = TILE_PROJ
    row = lambda n: pl.BlockSpec((T, n), lambda i: (i, 0))
    full = lambda a: pl.BlockSpec(a.shape, lambda i: (0,) * a.ndim)
    out_shapes = (
        jax.ShapeDtypeStruct((S, D_RNN), F32),
        jax.ShapeDtypeStruct((S, D_RNN), F32),
        jax.ShapeDtypeStruct((S, GLA_DK_TOTAL), F32),
        jax.ShapeDtypeStruct((S, GLA_DK_TOTAL), F32),
        jax.ShapeDtypeStruct((S, GLA_DV_TOTAL), BF16),
        jax.ShapeDtypeStruct((S, GLA_DV_TOTAL), F32),
        jax.ShapeDtypeStruct((S, 2 * GLA_DK_TOTAL), F32),
    )
    return pl.pallas_call(
        _inproj_kernel,
        grid=(S // T,),
        in_specs=[row(D_MODEL), full(gain), full(w_main), full(w_lr), full(w_gate), full(b_gate)],
        out_specs=[row(D_RNN), row(D_RNN), row(GLA_DK_TOTAL), row(GLA_DK_TOTAL),
                   row(GLA_DV_TOTAL), row(GLA_DV_TOTAL), row(2 * GLA_DK_TOTAL)],
        out_shape=out_shapes,
        compiler_params=pltpu.CompilerParams(
            dimension_semantics=("parallel",), vmem_limit_bytes=VMEM_LIMIT_BYTES),
        name="inproj",
    )(x, gain, w_main, w_lr, w_gate, b_gate)


def _lru_direction(x, prev8, next8, is_first, is_last, cw, cb, wa, ba, wx, bx, lam,
                   reverse, carry_ref, a_scr, h_scr, out_ref):
    T = x.shape[0]
    row = lax.broadcasted_iota(jnp.int32, x.shape, 0)
    prev8 = jnp.where(is_first, 0.0, prev8)
    next8 = jnp.where(is_last, 0.0, next8)
    p6, p7, n0 = prev8[6:7, :], prev8[7:8, :], next8[0:1, :]
    xm1 = jnp.where(row == 0, p7, pltpu.roll(x, 1, 0))
    xm2 = jnp.where(row == 0, p6, jnp.where(row == 1, p7, pltpu.roll(x, 2, 0)))
    xp1 = jnp.where(row == T - 1, n0, pltpu.roll(x, T - 1, 0))
    xc = cw[0:1, :] * xm2 + cw[1:2, :] * xm1 + cw[2:3, :] * x + cw[3:4, :] * xp1 + cb

    xcb = xc.astype(BF16)
    r = _sigmoid(_dot(xcb, wa) + ba)
    gi = _sigmoid(_dot(xcb, wx) + bx)
    log_a = (-LRU_C) * r * _softplus(-lam)
    a = jnp.exp(log_a)
    u = xc * gi * jnp.sqrt((1.0 + a * a) * jnp.tanh(-log_a))

    sub = jnp.bitwise_and(row, SUBLANES - 1)
    A, H = a, u
    for d in (1, 2, 4):
        if reverse:
            valid, shift = sub < SUBLANES - d, T - d
        else:
            valid, shift = sub >= d, d
        A_s = jnp.where(valid, pltpu.roll(A, shift, 0), 1.0)
        H_s = jnp.where(valid, pltpu.roll(H, shift, 0), 0.0)
        H = H + A * H_s
        A = A * A_s
    a_scr[...] = A
    h_scr[...] = H

    n_groups = T // SUBLANES

    def body(g, c):
        gi_ = (n_groups - 1 - g) if reverse else g
        off = pl.multiple_of(gi_ * SUBLANES, SUBLANES)
        hg = h_scr[pl.ds(off, SUBLANES), :] + a_scr[pl.ds(off, SUBLANES), :] * c
        out_ref[pl.ds(off, SUBLANES), :] = hg
        return hg[0:1, :] if reverse else hg[SUBLANES - 1:SUBLANES, :]

    carry_ref[...] = lax.fori_loop(0, n_groups, body, carry_ref[...], unroll=8)


def _lru_kernel(xf_ref, pf_ref, nf_ref, xb_ref, pb_ref, nb_ref,
                cw_ref, cb_ref, wa_ref, ba_ref, wx_ref, bx_ref, lam_ref,
                hf_ref, hb_ref, cf_ref, cbk_ref, a_scr, h_scr):
    i = pl.program_id(0)
    n = pl.num_programs(0)

    @pl.when(i == 0)
    def _():
        cf_ref[...] = jnp.zeros_like(cf_ref)
        cbk_ref[...] = jnp.zeros_like(cbk_ref)

    cw, cb = cw_ref[...], cb_ref[...]
    _lru_direction(xf_ref[...], pf_ref[...], nf_ref[...], i == 0, i == n - 1, cw, cb,
                   wa_ref[0], ba_ref[0:1, :], wx_ref[0], bx_ref[0:1, :], lam_ref[0:1, :],
                   False, cf_ref, a_scr, h_scr, hf_ref)
    _lru_direction(xb_ref[...], pb_ref[...], nb_ref[...], i == n - 1, i == 0, cw, cb,
                   wa_ref[1], ba_ref[1:2, :], wx_ref[1], bx_ref[1:2, :], lam_ref[1:2, :],
                   True, cbk_ref, a_scr, h_scr, hb_ref)


def _lru(xr, conv_w, conv_b, wa, ba, wx, bx, lam):
    S = xr.shape[0]
    T = TILE_LRU
    n = S // T
    hb_per_tile = T // HALO_ROWS
    n_halo = S // HALO_ROWS
    full = lambda a: pl.BlockSpec(a.shape, lambda i: (0,) * a.ndim)

    def tile(rev):
        t = (lambda i: n - 1 - i) if rev else (lambda i: i)
        cur = pl.BlockSpec((T, D_RNN), lambda i: (t(i), 0))
        prev = pl.BlockSpec((HALO_ROWS, D_RNN),
                            lambda i: (jnp.maximum(t(i) * hb_per_tile - 1, 0), 0))
        nxt = pl.BlockSpec((HALO_ROWS, D_RNN),
                           lambda i: (jnp.minimum((t(i) + 1) * hb_per_tile, n_halo - 1), 0))
        return cur, prev, nxt

    cf, pf, nf = tile(False)
    cbk, pb, nb = tile(True)
    return pl.pallas_call(
        _lru_kernel,
        grid=(n,),
        in_specs=[cf, pf, nf, cbk, pb, nb, full(conv_w), full(conv_b),
                  full(wa), full(ba), full(wx), full(bx), full(lam)],
        out_specs=[cf, cbk],
        out_shape=(jax.ShapeDtypeStruct((S, D_RNN), F32),) * 2,
        scratch_shapes=[pltpu.VMEM((1, D_RNN), F32), pltpu.VMEM((1, D_RNN), F32),
                        pltpu.VMEM((T, D_RNN), F32), pltpu.VMEM((T, D_RNN), F32)],
        compiler_params=pltpu.CompilerParams(
            dimension_semantics=("arbitrary",), vmem_limit_bytes=VMEM_LIMIT_BYTES),
        name="rglru",
    )(xr, xr, xr, xr, xr, xr, conv_w, conv_b, wa, ba, wx, bx, lam)


def _gla_direction(q_ref, k_ref, v_ref, la_ref, st_ref, out_ref, reverse):
    T = q_ref.shape[0]
    C = GLA_CHUNK
    la = la_ref[...]
    rowc = jnp.bitwise_and(lax.broadcasted_iota(jnp.int32, la.shape, 0), C - 1)
    b = la
    d = 1
    while d < C:
        if reverse:
            valid, shift = rowc < C - d, T - d
        else:
            valid, shift = rowc >= d, d
        b = b + jnp.where(valid, pltpu.roll(b, shift, 0), 0.0)
        d *= 2

    lane_k = lax.broadcasted_iota(jnp.int32, (C, GLA_DK_TOTAL), 1)
    lane_s = lax.broadcasted_iota(jnp.int32, (GLA_DV, GLA_DK_TOTAL), 1)
    ri = lax.broadcasted_iota(jnp.int32, (C, C), 0)
    ci = lax.broadcasted_iota(jnp.int32, (C, C), 1)
    causal = (ci >= ri) if reverse else (ci <= ri)

    n_chunks = T // C
    order = range(n_chunks - 1, -1, -1) if reverse else range(n_chunks)
    for c in order:
        rows = slice(c * C, (c + 1) * C)
        bc = b[rows, :]
        bl = bc[0:1, :] if reverse else bc[C - 1:C, :]
        qe = q_ref[rows, :] * jnp.exp(bc)
        kc = k_ref[rows, :]
        ke = (kc * jnp.exp(-bc)).astype(BF16)
        kd = (kc * jnp.exp(bl - bc)).astype(BF16)
        st = st_ref[...]
        rhs = jnp.concatenate([st.astype(BF16), ke], axis=0)
        upd = None
        for h in range(GLA_HEADS):
            in_head = (lane_k >= h * GLA_DK) & (lane_k < (h + 1) * GLA_DK)
            qm = jnp.where(in_head, qe, 0.0).astype(BF16)
            so = lax.dot_general(qm, rhs, (((1,), (1,)), ((), ())),
                                 preferred_element_type=F32)
            o_inter = so[:, :GLA_DV]
            scores = jnp.where(causal, so[:, GLA_DV:], 0.0).astype(BF16)
            vh = v_ref[rows, h * GLA_DV:(h + 1) * GLA_DV]
            out_ref[rows, h * GLA_DV:(h + 1) * GLA_DV] = o_inter + _dot(scores, vh)
            uh = lax.dot_general(vh, kd, (((0,), (0,)), ((), ())),
                                 preferred_element_type=F32)
            upd = uh if upd is None else jnp.where(lane_s >= h * GLA_DK, uh, upd)
        st_ref[...] = st * jnp.exp(bl) + upd


def _gla_kernel(qf_ref, kf_ref, vf_ref, laf_ref, qb_ref, kb_ref, vb_ref, lab_ref,
                of_ref, ob_ref, stf_ref, stb_ref):
    @pl.when(pl.program_id(0) == 0)
    def _():
        stf_ref[...] = jnp.zeros_like(stf_ref)
        stb_ref[...] = jnp.zeros_like(stb_ref)

    _gla_direction(qf_ref, kf_ref, vf_ref, laf_ref, stf_ref, of_ref, False)
    _gla_direction(qb_ref, kb_ref, vb_ref, lab_ref, stb_ref, ob_ref, True)


def _gla(q, k, v, la):
    S = q.shape[0]
    T = TILE_GLA
    n = S // T
    fwd = lambda width, col=0: pl.BlockSpec((T, width), lambda i: (i, col))
    bwd = lambda width, col=0: pl.BlockSpec((T, width), lambda i: (n - 1 - i, col))
    return pl.pallas_call(
        _gla_kernel,
        grid=(n,),
        in_specs=[fwd(GLA_DK_TOTAL), fwd(GLA_DK_TOTAL), fwd(GLA_DV_TOTAL), fwd(GLA_DK_TOTAL, 0),
                  bwd(GLA_DK_TOTAL), bwd(GLA_DK_TOTAL), bwd(GLA_DV_TOTAL), bwd(GLA_DK_TOTAL, 1)],
        out_specs=[fwd(GLA_DV_TOTAL), bwd(GLA_DV_TOTAL)],
        out_shape=(jax.ShapeDtypeStruct((S, GLA_DV_TOTAL), F32),) * 2,
        scratch_shapes=[pltpu.VMEM((GLA_DV, GLA_DK_TOTAL), F32)] * 2,
        compiler_params=pltpu.CompilerParams(
            dimension_semantics=("arbitrary",), vmem_limit_bytes=VMEM_LIMIT_BYTES),
        name="gla",
    )(q, k, v, la, q, k, v, la)


def _out_ffn_kernel(x_ref, hf_ref, hb_ref, gate_ref, of_ref, ob_ref, g_ref,
                    rnn_norm_ref, gla_norm_ref, wo_rnn_ref, wo_gla_ref, mix_post_ref,
                    ffn_pre_ref, wg_ref, wu_ref, wd_ref, ffn_post_ref, y_ref):
    y_rnn = _rms_norm((hf_ref[...] + hb_ref[...]) * gate_ref[...], rnn_norm_ref[...])
    m = _dot(y_rnn.astype(BF16), wo_rnn_ref[...])
    gla_gain = gla_norm_ref[...]
    for h in range(GLA_HEADS):
        cols = slice(h * GLA_DV, (h + 1) * GLA_DV)
        o = _rms_norm(of_ref[:, cols] + ob_ref[:, cols], gla_gain) * g_ref[:, cols]
        m = m + _dot(o.astype(BF16), wo_gla_ref[cols, :])
    x1 = x_ref[...] + _rms_norm(m, mix_post_ref[...])

    h2 = _rms_norm(x1, ffn_pre_ref[...]).astype(BF16)
    a = _dot(h2, wg_ref[...])
    u = _dot(h2, wu_ref[...])
    z = (a * _sigmoid(a) * u).astype(BF16)
    f = _dot(z, wd_ref[...])
    y_ref[...] = x1 + _rms_norm(f, ffn_post_ref[...])


def _out_ffn(x, hf, hb, gate, of, ob, g, rnn_norm, gla_norm, wo_rnn, wo_gla, mix_post,
             ffn_pre, wg, wu, wd, ffn_post):
    S = x.shape[0]
    T = TILE_OUT
    row = lambda n: pl.BlockSpec((T, n), lambda i: (i, 0))
    const = lambda a: pl.BlockSpec(a.shape, lambda i: (0,) * a.ndim,
                                   pipeline_mode=pl.Buffered(1))
    return pl.pallas_call(
        _out_ffn_kernel,
        grid=(S // T,),
        in_specs=[row(D_MODEL), row(D_RNN), row(D_RNN), row(D_RNN), row(GLA_DV_TOTAL),
                  row(GLA_DV_TOTAL), row(GLA_DV_TOTAL), const(rnn_norm), const(gla_norm),
                  const(wo_rnn), const(wo_gla), const(mix_post), const(ffn_pre),
                  const(wg), const(wu), const(wd), const(ffn_post)],
        out_specs=row(D_MODEL),
        out_shape=jax.ShapeDtypeStruct((S, D_MODEL), F32),
        compiler_params=pltpu.CompilerParams(
            dimension_semantics=("parallel",), vmem_limit_bytes=VMEM_LIMIT_BYTES),
        name="out_ffn",
    )(x, hf, hb, gate, of, ob, g, rnn_norm, gla_norm, wo_rnn, wo_gla, mix_post,
      ffn_pre, wg, wu, wd, ffn_post)


def _block_diag(w):
    eye = jnp.eye(RNN_HEADS, dtype=w.dtype)
    return jnp.einsum('hij,hg->higj', w, eye).reshape(D_RNN, D_RNN)


def _row2(v):
    return v.reshape(1, -1).astype(F32)


def kernel(x, mix_norm_pre, mix_norm_post, w_in, conv_w, conv_b, lru_w_a, lru_b_a, lru_w_x,
           lru_b_x, lru_lambda, rnn_out_norm, gla_w_gate, gla_b_gate, gla_out_norm, w_out,
           ffn_norm_pre, ffn_norm_post, w_ffn_gate, w_ffn_up, w_ffn_down):
    B, S, D = x.shape
    depth = w_in.shape[0]
    xs = x.reshape(B * S, D)
    assert B == 1
    for l in range(depth):
        w_main = w_in[l][:, :_P_END].astype(BF16)
        w_lr = jnp.pad(w_in[l][:, _P_END:], ((0, 0), (0, LANES - 2 * GLA_RANK))).astype(BF16)
        w_gate = jnp.zeros((LANES, 2 * GLA_DK_TOTAL), F32)
        w_gate = w_gate.at[:GLA_RANK, :GLA_DK_TOTAL].set(gla_w_gate[l, 0])
        w_gate = w_gate.at[GLA_RANK:2 * GLA_RANK, GLA_DK_TOTAL:].set(gla_w_gate[l, 1])
        b_gate = gla_b_gate[l].reshape(1, 2 * GLA_DK_TOTAL)

        xr, gate, q, k, v, g, la = _inproj(xs, _row2(mix_norm_pre[l]), w_main, w_lr,
                                           w_gate.astype(BF16), b_gate)

        wa = jnp.stack([_block_diag(lru_w_a[l, 0]), _block_diag(lru_w_a[l, 1])]).astype(BF16)
        wx = jnp.stack([_block_diag(lru_w_x[l, 0]), _block_diag(lru_w_x[l, 1])]).astype(BF16)
        hf, hb = _lru(xr, conv_w[l], _row2(conv_b[l]), wa, lru_b_a[l], wx, lru_b_x[l],
                      lru_lambda[l])

        of, ob = _gla(q, k, v, la)

        xs = _out_ffn(xs, hf, hb, gate, of, ob, g, _row2(rnn_out_norm[l]),
                      _row2(gla_out_norm[l]), w_out[l][:D_RNN].astype(BF16),
                      w_out[l][D_RNN:].astype(BF16), _row2(mix_norm_post[l]),
                      _row2(ffn_norm_pre[l]), w_ffn_gate[l].astype(BF16),
                      w_ffn_up[l].astype(BF16), w_ffn_down[l].astype(BF16),
                      _row2(ffn_norm_post[l]))
    return xs.reshape(B, S, D)
```

```python
import functools

import jax
import jax.numpy as jnp
from jax import lax
from jax.experimental import pallas as pl
from jax.experimental.pallas import tpu as pltpu

F32 = jnp.float32
BF16 = jnp.bfloat16

D_MODEL = 1024
D_RNN = 512
RNN_HEADS = 8
RNN_BLOCK = D_RNN // RNN_HEADS
CONV_WIDTH = 4
LRU_C = 8.0
GLA_HEADS = 4
GLA_DV_TOTAL = 512
GLA_DK_TOTAL = 256
GLA_DV = GLA_DV_TOTAL // GLA_HEADS
GLA_DK = GLA_DK_TOTAL // GLA_HEADS
GLA_RANK = 16
GLA_TAU = 16.0
GLA_CHUNK = 64
RMS_EPS = 1e-6

SUBLANES = 8
LANES = 128
HALO_ROWS = SUBLANES
VMEM_LIMIT_BYTES = 56 * 1024 * 1024

TILE_PROJ = 512
TILE_MIX = 512
TILE_OUT = 256
LRU_BLOCK = 128
GLA_CHUNKS_PER_PIECE = 2
_DONE = object()

_P_XR = 0
_P_GR = _P_XR + D_RNN
_P_Q = _P_GR + D_RNN
_P_K = _P_Q + GLA_DK_TOTAL
_P_V = _P_K + GLA_DK_TOTAL
_P_G = _P_V + GLA_DV_TOTAL
_P_END = _P_G + GLA_DV_TOTAL


def _rms_norm(x, gain):
    ms = jnp.mean(x * x, axis=-1, keepdims=True)
    return x * lax.rsqrt(ms + RMS_EPS) * gain


def _sigmoid(x):
    return 1.0 / (1.0 + jnp.exp(-x))


def _softplus(x):
    return jnp.maximum(x, 0.0) + jnp.log1p(jnp.exp(-jnp.abs(x)))


def _gelu_tanh(x):
    c = 0.7978845608028654
    return 0.5 * x * (1.0 + jnp.tanh(c * (x + 0.044715 * (x * x * x))))


def _dot(a, b):
    return jnp.dot(a, b, preferred_element_type=F32)


def _inproj_kernel(x_ref, gain_ref, w_ref, wlr_ref, wg_ref, bg_ref,
                   xr_ref, gate_ref, q_ref, k_ref, v_ref, g_ref, la_ref):
    h = _rms_norm(x_ref[...], gain_ref[...]).astype(BF16)
    xr_ref[...] = _dot(h, w_ref[:, _P_XR:_P_GR])
    gate_ref[...] = _gelu_tanh(_dot(h, w_ref[:, _P_GR:_P_Q]))
    q_ref[...] = _dot(h, w_ref[:, _P_Q:_P_K]) * (GLA_DK ** -0.5)
    k_ref[...] = _dot(h, w_ref[:, _P_K:_P_V])
    v_ref[...] = _dot(h, w_ref[:, _P_V:_P_G]).astype(BF16)
    g = _dot(h, w_ref[:, _P_G:_P_END])
    g_ref[...] = g * _sigmoid(g)
    lr = _dot(h, wlr_ref[...]).astype(BF16)
    z = _dot(lr, wg_ref[...]) + bg_ref[...]
    la_ref[...] = -_softplus(-z) * (1.0 / GLA_TAU)


def _inproj(x, gain, w_main, w_lr, w_gate, b_gate):
    S = x.shape[0]
    T = TILE_PROJ
    row = lambda n: pl.BlockSpec((T, n), lambda i: (i, 0))
    full = lambda a: pl.BlockSpec(a.shape, lambda i: (0,) * a.ndim)
    out_shapes = (
        jax.ShapeDtypeStruct((S, D_RNN), F32),
        jax.ShapeDtypeStruct((S, D_RNN), F32),
        jax.ShapeDtypeStruct((S, GLA_DK_TOTAL), F32),
        jax.ShapeDtypeStruct((S, GLA_DK_TOTAL), F32),
        jax.ShapeDtypeStruct((S, GLA_DV_TOTAL), BF16),
        jax.ShapeDtypeStruct((S, GLA_DV_TOTAL), F32),
        jax.ShapeDtypeStruct((S, 2 * GLA_DK_TOTAL), F32),
    )
    return pl.pallas_call(
        _inproj_kernel,
        grid=(S // T,),
        in_specs=[row(D_MODEL), full(gain), full(w_main), full(w_lr), full(w_gate), full(b_gate)],
        out_specs=[row(D_RNN), row(D_RNN), row(GLA_DK_TOTAL), row(GLA_DK_TOTAL),
                   row(GLA_DV_TOTAL), row(GLA_DV_TOTAL), row(2 * GLA_DK_TOTAL)],
        out_shape=out_shapes,
        compiler_params=pltpu.CompilerParams(
            dimension_semantics=("parallel",), vmem_limit_bytes=VMEM_LIMIT_BYTES),
        name="inproj",
    )(x, gain, w_main, w_lr, w_gate, b_gate)


def _lru_pieces(x_ref, prev_ref, next_ref, is_first, is_last, cw, cb, wa, ba, wx, bx, lam,
                reverse, carry_ref, out_ref):
    T = x_ref.shape[0]
    R = LRU_BLOCK
    n_blocks = T // R
    n_groups = R // SUBLANES
    grouped = (n_groups, SUBLANES, x_ref.shape[1])
    sub = lax.broadcasted_iota(jnp.int32, grouped, 1)
    sp = _softplus(-lam)
    k_exp2 = (-LRU_C * 1.4426950408889634) * sp
    k_tanh = LRU_C * sp
    c = carry_ref[...]
    for blk in (range(n_blocks - 1, -1, -1) if reverse else range(n_blocks)):
        lo, hi = blk * R, (blk + 1) * R
        xg = x_ref[lo:hi, :].reshape(grouped)
        if blk == 0:
            prev8 = jnp.where(is_first, 0.0, prev_ref[...])[None]
        else:
            prev8 = x_ref[lo - SUBLANES:lo, :][None]
        if blk == n_blocks - 1:
            next8 = jnp.where(is_last, 0.0, next_ref[...])[None]
        else:
            next8 = x_ref[hi:hi + SUBLANES, :][None]

        def shifted(k):
            rot = pltpu.roll(xg, (-k) % SUBLANES, 1)
            if k < 0:
                edge = pltpu.roll(prev8, (-k) % SUBLANES, 1)
                return jnp.where(sub >= -k, rot, jnp.concatenate([edge, rot[:-1]], axis=0))
            edge = pltpu.roll(next8, (-k) % SUBLANES, 1)
            return jnp.where(sub < SUBLANES - k, rot, jnp.concatenate([rot[1:], edge], axis=0))

        xc = (cw[0:1, :] * shifted(-2) + cw[1:2, :] * shifted(-1) + cw[2:3, :] * xg
              + cw[3:4, :] * shifted(1) + cb).reshape(R, -1)
        xcb = xc.astype(BF16)
        r = _sigmoid(_dot(xcb, wa) + ba)
        gi = _sigmoid(_dot(xcb, wx) + bx)
        a = jnp.exp2(r * k_exp2)
        z = (1.0 + a * a) * jnp.tanh(r * k_tanh)
        u = xc * gi * jnp.where(z > 0.0, z * lax.rsqrt(z), 0.0)

        A, H = a.reshape(grouped), u.reshape(grouped)
        for d in (1, 2, 4):
            if reverse:
                valid, shift = sub < SUBLANES - d, SUBLANES - d
            else:
                valid, shift = sub >= d, d
            A_s = jnp.where(valid, pltpu.roll(A, shift, 1), 1.0)
            H_s = jnp.where(valid, pltpu.roll(H, shift, 1), 0.0)
            H = H + A * H_s
            A = A * A_s

        for g in (range(n_groups - 1, -1, -1) if reverse else range(n_groups)):
            hg = H[g] + A[g] * c
            out_ref[lo + g * SUBLANES:lo + (g + 1) * SUBLANES, :] = hg
            c = hg[0:1, :] if reverse else hg[SUBLANES - 1:SUBLANES, :]
        yield
    carry_ref[...] = c


def _gla_pieces(q_ref, k_ref, v_ref, la_ref, st_ref, out_ref, reverse):
    T = q_ref.shape[0]
    C = GLA_CHUNK
    la = la_ref[...]
    rowc = jnp.bitwise_and(lax.broadcasted_iota(jnp.int32, la.shape, 0), C - 1)
    b = la
    d = 1
    while d < C:
        if reverse:
            valid, shift = rowc < C - d, T - d
        else:
            valid, shift = rowc >= d, d
        b = b + jnp.where(valid, pltpu.roll(b, shift, 0), 0.0)
        d *= 2

    lane_k = lax.broadcasted_iota(jnp.int32, (C, GLA_DK_TOTAL), 1)
    lane_s = lax.broadcasted_iota(jnp.int32, (GLA_DV, GLA_DK_TOTAL), 1)
    ri = lax.broadcasted_iota(jnp.int32, (C, C), 0)
    ci = lax.broadcasted_iota(jnp.int32, (C, C), 1)
    causal = (ci >= ri) if reverse else (ci <= ri)

    n_chunks = T // C
    order = list(range(n_chunks - 1, -1, -1) if reverse else range(n_chunks))
    rows = [slice(c * C, (c + 1) * C) for c in range(n_chunks)]

    qe, ke, dl, upd = {}, {}, {}, {}
    for n_done, c in enumerate(order):
        if n_done and n_done % GLA_CHUNKS_PER_PIECE == 0:
            yield
        bc = b[rows[c], :]
        bl = bc[0:1, :] if reverse else bc[C - 1:C, :]
        qe[c] = q_ref[rows[c], :] * jnp.exp(bc)
        kc = k_ref[rows[c], :]
        ke[c] = (kc * jnp.exp(-bc)).astype(BF16)
        kd = (kc * jnp.exp(bl - bc)).astype(BF16)
        dl[c] = jnp.exp(bl)
        p = lax.dot_general(v_ref[rows[c], :], kd, (((0,), (0,)), ((), ())),
                            preferred_element_type=F32)
        u = p[:GLA_DV, :]
        for h in range(1, GLA_HEADS):
            u = jnp.where(lane_s >= h * GLA_DK, p[h * GLA_DV:(h + 1) * GLA_DV, :], u)
        upd[c] = u

    st = st_ref[...]
    st_in = {}
    for c in order:
        st_in[c] = st.astype(BF16)
        st = st * dl[c] + upd[c]
    st_ref[...] = st

    for n_done, c in enumerate(order):
        if n_done % GLA_CHUNKS_PER_PIECE == 0:
            yield
        rhs = jnp.concatenate([st_in[c], ke[c]], axis=0)
        lhs = jnp.concatenate(
            [jnp.where((lane_k >= h * GLA_DK) & (lane_k < (h + 1) * GLA_DK), qe[c], 0.0)
             .astype(BF16) for h in range(GLA_HEADS)], axis=0)
        so = lax.dot_general(lhs, rhs, (((1,), (1,)), ((), ())),
                             preferred_element_type=F32)
        for h in range(GLA_HEADS):
            soh = so[h * C:(h + 1) * C, :]
            scores = jnp.where(causal, soh[:, GLA_DV:], 0.0).astype(BF16)
            vh = v_ref[rows[c], h * GLA_DV:(h + 1) * GLA_DV]
            out_ref[rows[c], h * GLA_DV:(h + 1) * GLA_DV] = soh[:, :GLA_DV] + _dot(scores, vh)


def _mixer_kernel(xf_ref, pf_ref, nf_ref, qf_ref, kf_ref, vf_ref, laf_ref,
                  xb_ref, pb_ref, nb_ref, qb_ref, kb_ref, vb_ref, lab_ref,
                  cw_ref, cb_ref, wa_ref, ba_ref, wx_ref, bx_ref, lam_ref,
                  hf_ref, hb_ref, of_ref, ob_ref,
                  cf_ref, cbk_ref, stf_ref, stb_ref):
    i = pl.program_id(0)
    n = pl.num_programs(0)

    @pl.when(i == 0)
    def _():
        cf_ref[...] = jnp.zeros_like(cf_ref)
        cbk_ref[...] = jnp.zeros_like(cbk_ref)
        stf_ref[...] = jnp.zeros_like(stf_ref)
        stb_ref[...] = jnp.zeros_like(stb_ref)

    cw, cb = cw_ref[...], cb_ref[...]
    streams = [
        _gla_pieces(qf_ref, kf_ref, vf_ref, laf_ref, stf_ref, of_ref, False),
        _lru_pieces(xf_ref, pf_ref, nf_ref, i == 0, i == n - 1, cw, cb,
                    wa_ref[0], ba_ref[0:1, :], wx_ref[0], bx_ref[0:1, :], lam_ref[0:1, :],
                    False, cf_ref, hf_ref),
        _gla_pieces(qb_ref, kb_ref, vb_ref, lab_ref, stb_ref, ob_ref, True),
        _lru_pieces(xb_ref, pb_ref, nb_ref, i == n - 1, i == 0, cw, cb,
                    wa_ref[1], ba_ref[1:2, :], wx_ref[1], bx_ref[1:2, :], lam_ref[1:2, :],
                    True, cbk_ref, hb_ref),
    ]
    while streams:
        for s in list(streams):
            if next(s, _DONE) is _DONE:
                streams.remove(s)


def _mixer(xr, q, k, v, la, conv_w, conv_b, wa, ba, wx, bx, lam):
    S = xr.shape[0]
    T = TILE_MIX
    n = S // T
    hb_per_tile = T // HALO_ROWS
    n_halo = S // HALO_ROWS
    full = lambda a: pl.BlockSpec(a.shape, lambda i: (0,) * a.ndim)

    def specs(rev):
        t = (lambda i: n - 1 - i) if rev else (lambda i: i)
        blk = lambda width, col=0: pl.BlockSpec((T, width), lambda i: (t(i), col))
        prev = pl.BlockSpec((HALO_ROWS, D_RNN),
                            lambda i: (jnp.maximum(t(i) * hb_per_tile - 1, 0), 0))
        nxt = pl.BlockSpec((HALO_ROWS, D_RNN),
                           lambda i: (jnp.minimum((t(i) + 1) * hb_per_tile, n_halo - 1), 0))
        ins = [blk(D_RNN), prev, nxt, blk(GLA_DK_TOTAL), blk(GLA_DK_TOTAL), blk(GLA_DV_TOTAL),
               blk(GLA_DK_TOTAL, 1 if rev else 0)]
        return ins, blk(D_RNN), blk(GLA_DV_TOTAL)

    ins_f, h_f, o_f = specs(False)
    ins_b, h_b, o_b = specs(True)
    params = (conv_w, conv_b, wa, ba, wx, bx, lam)
    return pl.pallas_call(
        _mixer_kernel,
        grid=(n,),
        in_specs=ins_f + ins_b + [full(p) for p in params],
        out_specs=[h_f, h_b, o_f, o_b],
        out_shape=(jax.ShapeDtypeStruct((S, D_RNN), F32),) * 2
                  + (jax.ShapeDtypeStruct((S, GLA_DV_TOTAL), F32),) * 2,
        scratch_shapes=[pltpu.VMEM((1, D_RNN), F32)] * 2
                       + [pltpu.VMEM((GLA_DV, GLA_DK_TOTAL), F32)] * 2,
        compiler_params=pltpu.CompilerParams(
            dimension_semantics=("arbitrary",), vmem_limit_bytes=VMEM_LIMIT_BYTES),
        name="mixer",
    )(xr, xr, xr, q, k, v, la, xr, xr, xr, q, k, v, la, *params)


def _out_ffn_kernel(x_ref, hf_ref, hb_ref, gate_ref, of_ref, ob_ref, g_ref,
                    rnn_norm_ref, gla_norm_ref, wo_rnn_ref, wo_gla_ref, mix_post_ref,
                    ffn_pre_ref, wg_ref, wu_ref, wd_ref, ffn_post_ref, y_ref):
    y_rnn = _rms_norm((hf_ref[...] + hb_ref[...]) * gate_ref[...], rnn_norm_ref[...])
    m = _dot(y_rnn.astype(BF16), wo_rnn_ref[...])
    gla_gain = gla_norm_ref[...]
    for h in range(GLA_HEADS):
        cols = slice(h * GLA_DV, (h + 1) * GLA_DV)
        o = _rms_norm(of_ref[:, cols] + ob_ref[:, cols], gla_gain) * g_ref[:, cols]
        m = m + _dot(o.astype(BF16), wo_gla_ref[cols, :])
    x1 = x_ref[...] + _rms_norm(m, mix_post_ref[...])

    h2 = _rms_norm(x1, ffn_pre_ref[...]).astype(BF16)
    a = _dot(h2, wg_ref[...])
    u = _dot(h2, wu_ref[...])
    z = (a * _sigmoid(a) * u).astype(BF16)
    f = _dot(z, wd_ref[...])
    y_ref[...] = x1 + _rms_norm(f, ffn_post_ref[...])


def _out_ffn(x, hf, hb, gate, of, ob, g, rnn_norm, gla_norm, wo_rnn, wo_gla, mix_post,
             ffn_pre, wg, wu, wd, ffn_post):
    S = x.shape[0]
    T = TILE_OUT
    row = lambda n: pl.BlockSpec((T, n), lambda i: (i, 0))
    const = lambda a: pl.BlockSpec(a.shape, lambda i: (0,) * a.ndim,
                                   pipeline_mode=pl.Buffered(1))
    return pl.pallas_call(
        _out_ffn_kernel,
        grid=(S // T,),
        in_specs=[row(D_MODEL), row(D_RNN), row(D_RNN), row(D_RNN), row(GLA_DV_TOTAL),
                  row(GLA_DV_TOTAL), row(GLA_DV_TOTAL), const(rnn_norm), const(gla_norm),
                  const(wo_rnn), const(wo_gla), const(mix_post), const(ffn_pre),
                  const(wg), const(wu), const(wd), const(ffn_post)],
        out_specs=row(D_MODEL),
        out_shape=jax.ShapeDtypeStruct((S, D_MODEL), F32),
        compiler_params=pltpu.CompilerParams(
            dimension_semantics=("parallel",), vmem_limit_bytes=VMEM_LIMIT_BYTES),
        name="out_ffn",
    )(x, hf, hb, gate, of, ob, g, rnn_norm, gla_norm, wo_rnn, wo_gla, mix_post,
      ffn_pre, wg, wu, wd, ffn_post)


def _block_diag(w):
    eye = jnp.eye(RNN_HEADS, dtype=w.dtype)
    return jnp.einsum('hij,hg->higj', w, eye).reshape(D_RNN, D_RNN)


def _row2(v):
    return v.reshape(1, -1).astype(F32)


def kernel(x, mix_norm_pre, mix_norm_post, w_in, conv_w, conv_b, lru_w_a, lru_b_a, lru_w_x,
           lru_b_x, lru_lambda, rnn_out_norm, gla_w_gate, gla_b_gate, gla_out_norm, w_out,
           ffn_norm_pre, ffn_norm_post, w_ffn_gate, w_ffn_up, w_ffn_down):
    B, S, D = x.shape
    depth = w_in.shape[0]
    xs = x.reshape(B * S, D)
    assert B == 1
    for l in range(depth):
        w_main = w_in[l][:, :_P_END].astype(BF16)
        w_lr = jnp.pad(w_in[l][:, _P_END:], ((0, 0), (0, LANES - 2 * GLA_RANK))).astype(BF16)
        w_gate = jnp.zeros((LANES, 2 * GLA_DK_TOTAL), F32)
        w_gate = w_gate.at[:GLA_RANK, :GLA_DK_TOTAL].set(gla_w_gate[l, 0])
        w_gate = w_gate.at[GLA_RANK:2 * GLA_RANK, GLA_DK_TOTAL:].set(gla_w_gate[l, 1])
        b_gate = gla_b_gate[l].reshape(1, 2 * GLA_DK_TOTAL)

        xr, gate, q, k, v, g, la = _inproj(xs, _row2(mix_norm_pre[l]), w_main, w_lr,
                                           w_gate.astype(BF16), b_gate)

        wa = jnp.stack([_block_diag(lru_w_a[l, 0]), _block_diag(lru_w_a[l, 1])]).astype(BF16)
        wx = jnp.stack([_block_diag(lru_w_x[l, 0]), _block_diag(lru_w_x[l, 1])]).astype(BF16)
        hf, hb, of, ob = _mixer(xr, q, k, v, la, conv_w[l], _row2(conv_b[l]), wa, lru_b_a[l],
                                wx, lru_b_x[l], lru_lambda[l])

        xs = _out_ffn(xs, hf, hb, gate, of, ob, g, _row2(rnn_out_norm[l]),
                      _row2(gla_out_norm[l]), w_out[l][:D_RNN].astype(BF16),
                      w_out[l][D_RNN:].astype(BF16), _row2(mix_norm_post[l]),
                      _row2(ffn_norm_pre[l]), w_ffn_gate[l].astype(BF16),
                      w_ffn_up[l].astype(BF16), w_ffn_down[l].astype(BF16),
                      _row2(ffn_norm_post[l]))
    return xs.reshape(B, S, D)
```

```python
import functools

import jax
import jax.numpy as jnp
from jax import lax
from jax.experimental import pallas as pl
from jax.experimental.pallas import tpu as pltpu

F32 = jnp.float32
BF16 = jnp.bfloat16

D_MODEL = 1024
D_RNN = 512
RNN_HEADS = 8
RNN_BLOCK = D_RNN // RNN_HEADS
CONV_WIDTH = 4
LRU_C = 8.0
GLA_HEADS = 4
GLA_DV_TOTAL = 512
GLA_DK_TOTAL = 256
GLA_DV = GLA_DV_TOTAL // GLA_HEADS
GLA_DK = GLA_DK_TOTAL // GLA_HEADS
GLA_RANK = 16
GLA_TAU = 16.0
GLA_CHUNK = 64
RMS_EPS = 1e-6

SUBLANES = 8
LANES = 128
HALO_ROWS = SUBLANES
VMEM_LIMIT_BYTES = 56 * 1024 * 1024

TILE_PROJ = 512
PROJ_ROWS = 256
TILE_MIX = 512
TILE_OUT = 512
OUT_ROWS = 256
FFN_CHUNK = 256
LRU_BLOCK = 128
GLA_CHUNKS_PER_PIECE = 1
_DONE = object()

_P_XR = 0
_P_GR = _P_XR + D_RNN
_P_Q = _P_GR + D_RNN
_P_K = _P_Q + GLA_DK_TOTAL
_P_V = _P_K + GLA_DK_TOTAL
_P_G = _P_V + GLA_DV_TOTAL
_P_END = _P_G + GLA_DV_TOTAL


def _rms_norm(x, gain):
    ms = jnp.mean(x * x, axis=-1, keepdims=True)
    return x * lax.rsqrt(ms + RMS_EPS) * gain


def _sigmoid(x):
    return 1.0 / (1.0 + jnp.exp(-x))


def _softplus(x):
    return jnp.maximum(x, 0.0) + jnp.log1p(jnp.exp(-jnp.abs(x)))


def _log_sigmoid(x):
    return jnp.minimum(x, 0.0) - jnp.log(1.0 + jnp.exp(-jnp.abs(x)))


def _gelu_tanh(x):
    c = 0.7978845608028654
    half = 0.5 * x
    return half + half * jnp.tanh(x * (c + (c * 0.044715) * (x * x)))


def _dot(a, b):
    return jnp.dot(a, b, preferred_element_type=F32)


def _layer_spec(a, layer, **kwargs):
    return pl.BlockSpec((None,) + a.shape[1:], lambda i: (layer,) + (0,) * (a.ndim - 1), **kwargs)


def _round_robin(streams):
    streams = list(streams)
    while streams:
        for s in list(streams):
            if next(s, _DONE) is _DONE:
                streams.remove(s)


def _gla_operands(q, k, la, reverse, r0, qe_ref, ke_ref, kd_ref, dl_ref):
    R, C = la.shape[0], GLA_CHUNK
    rowc = jnp.bitwise_and(lax.broadcasted_iota(jnp.int32, la.shape, 0), C - 1)
    b = la
    d = 1
    while d < C:
        if reverse:
            valid, shift = rowc < C - d, R - d
        else:
            valid, shift = rowc >= d, d
        b = b + jnp.where(valid, pltpu.roll(b, shift, 0), 0.0)
        d *= 2
    for c in range(R // C):
        lo = c * C
        bc = b[lo:lo + C, :]
        bl = bc[0:1, :] if reverse else bc[C - 1:C, :]
        qe_ref[r0 + lo:r0 + lo + C, :] = (q[lo:lo + C, :] * jnp.exp(bc)).astype(BF16)
        ke_ref[r0 + lo:r0 + lo + C, :] = (k[lo:lo + C, :] * jnp.exp(-bc)).astype(BF16)
        kd_ref[r0 + lo:r0 + lo + C, :] = (k[lo:lo + C, :] * jnp.exp(bl - bc)).astype(BF16)
        ci = (r0 + lo) // C
        dl_ref[ci:ci + 1, :] = jnp.exp(bl)


def _inproj_pieces(r0, x_ref, xp_ref, xn_ref, gain_ref, w_ref, wlr_ref, wg_ref, bg_ref,
                   cw_ref, cb_ref, xc_ref, gate_ref, v_ref, g_ref,
                   qef_ref, kef_ref, kdf_ref, dlf_ref, qeb_ref, keb_ref, kdb_ref, dlb_ref):
    T, R = x_ref.shape[0], PROJ_ROWS
    rows = slice(r0, r0 + R)
    i, n = pl.program_id(0), pl.num_programs(0)
    if r0 == 0:
        x_prev, pad_prev = xp_ref[...], i == 0
    else:
        x_prev, pad_prev = x_ref[r0 - SUBLANES:r0, :], False
    if r0 + R == T:
        x_next, pad_next = xn_ref[...], i == n - 1
    else:
        x_next, pad_next = x_ref[r0 + R:r0 + R + SUBLANES, :], False
    x_ext = jnp.concatenate([x_ref[rows, :], x_prev, x_next], axis=0)
    h_ext = _rms_norm(x_ext, gain_ref[...]).astype(BF16)
    h = h_ext[:R, :]
    yield
    lr = _dot(h, wlr_ref[...]).astype(BF16)
    xr = _dot(h_ext, w_ref[:, _P_XR:_P_GR])
    yield
    z = _dot(lr, wg_ref[...]) + bg_ref[...]
    gate = _dot(h, w_ref[:, _P_GR:_P_Q])
    grouped = (R // SUBLANES, SUBLANES, xr.shape[1])
    sub = lax.broadcasted_iota(jnp.int32, grouped, 1)
    xg = xr[:R, :].reshape(grouped)
    prev8 = jnp.where(pad_prev, 0.0, xr[R:R + SUBLANES, :])[None]
    next8 = jnp.where(pad_next, 0.0, xr[R + SUBLANES:, :])[None]

    def shifted(k):
        rot = pltpu.roll(xg, (-k) % SUBLANES, 1)
        if k < 0:
            edge = pltpu.roll(prev8, (-k) % SUBLANES, 1)
            return jnp.where(sub >= -k, rot, jnp.concatenate([edge, rot[:-1]], axis=0))
        edge = pltpu.roll(next8, (-k) % SUBLANES, 1)
        return jnp.where(sub < SUBLANES - k, rot, jnp.concatenate([rot[1:], edge], axis=0))

    cw = cw_ref[...]
    xc_ref[rows, :] = (cw[0:1, :] * shifted(-2) + cw[1:2, :] * shifted(-1) + cw[2:3, :] * xg
                       + cw[3:4, :] * shifted(1) + cb_ref[...]).reshape(R, -1)
    yield
    q = _dot(h, w_ref[:, _P_Q:_P_K]) * (GLA_DK ** -0.5)
    k = _dot(h, w_ref[:, _P_K:_P_V])
    la = _log_sigmoid(z) * (1.0 / GLA_TAU)
    yield
    g = _dot(h, w_ref[:, _P_G:_P_END])
    gate_ref[rows, :] = _gelu_tanh(gate)
    _gla_operands(q, k, la[:, :GLA_DK_TOTAL], False, r0, qef_ref, kef_ref, kdf_ref, dlf_ref)
    yield
    v_ref[rows, :] = _dot(h, w_ref[:, _P_V:_P_G]).astype(BF16)
    g_ref[rows, :] = g * _sigmoid(g)
    _gla_operands(q, k, la[:, GLA_DK_TOTAL:], True, r0, qeb_ref, keb_ref, kdb_ref, dlb_ref)


def _inproj_kernel(*refs):
    T = refs[0].shape[0]
    _round_robin(_inproj_pieces(r0, *refs) for r0 in range(0, T, PROJ_ROWS))


def _inproj(x, layer, gain, w_main, w_lr, w_gate, b_gate, conv_w, conv_b):
    S = x.shape[0]
    T = TILE_PROJ
    halo_per_tile = T // HALO_ROWS
    n_halo = S // HALO_ROWS
    row = lambda n: pl.BlockSpec((T, n), lambda i: (i, 0))
    full = lambda a: _layer_spec(a, layer)
    prev = pl.BlockSpec((HALO_ROWS, D_MODEL),
                        lambda i: (jnp.maximum(i * halo_per_tile - 1, 0), 0))
    nxt = pl.BlockSpec((HALO_ROWS, D_MODEL),
                       lambda i: (jnp.minimum((i + 1) * halo_per_tile, n_halo - 1), 0))
    dl_spec = pl.BlockSpec((T // GLA_CHUNK, GLA_DK_TOTAL), lambda i: (i, 0))
    gla_shapes = (
        jax.ShapeDtypeStruct((S, GLA_DK_TOTAL), BF16),
        jax.ShapeDtypeStruct((S, GLA_DK_TOTAL), BF16),
        jax.ShapeDtypeStruct((S, GLA_DK_TOTAL), BF16),
        jax.ShapeDtypeStruct((S // GLA_CHUNK, GLA_DK_TOTAL), F32),
    )
    gla_specs = [row(GLA_DK_TOTAL)] * 3 + [dl_spec]
    out_shapes = (
        jax.ShapeDtypeStruct((S, D_RNN), F32),
        jax.ShapeDtypeStruct((S, D_RNN), F32),
        jax.ShapeDtypeStruct((S, GLA_DV_TOTAL), BF16),
        jax.ShapeDtypeStruct((S, GLA_DV_TOTAL), F32),
    ) + gla_shapes + gla_shapes
    params = (gain, w_main, w_lr, w_gate, b_gate, conv_w, conv_b)
    return pl.pallas_call(
        _inproj_kernel,
        grid=(S // T,),
        in_specs=[row(D_MODEL), prev, nxt] + [full(p) for p in params],
        out_specs=[row(D_RNN), row(D_RNN), row(GLA_DV_TOTAL), row(GLA_DV_TOTAL)]
                  + gla_specs + gla_specs,
        out_shape=out_shapes,
        compiler_params=pltpu.CompilerParams(
            dimension_semantics=("parallel",), vmem_limit_bytes=VMEM_LIMIT_BYTES),
        name="inproj",
    )(x, x, x, *params)


def _lru_pieces(xc_ref, wa, ba, wx, bx, lam, reverse, carry_ref, out_ref):
    T = xc_ref.shape[0]
    R = LRU_BLOCK
    n_groups = R // SUBLANES
    grouped = (n_groups, SUBLANES, xc_ref.shape[1])
    sub = lax.broadcasted_iota(jnp.int32, grouped, 1)
    sp = _softplus(-lam)
    k_exp2 = (-LRU_C * 1.4426950408889634) * sp
    k_tanh = LRU_C * sp
    backwards = lambda n: range(n - 1, -1, -1) if reverse else range(n)
    c = carry_ref[...]
    for blk in backwards(T // R):
        lo = blk * R
        xc = xc_ref[lo:lo + R, :]
        xcb = xc.astype(BF16)
        r = _sigmoid(_dot(xcb, wa) + ba)
        gi = _sigmoid(_dot(xcb, wx) + bx)
        a = jnp.exp2(r * k_exp2)
        z = (1.0 + a * a) * jnp.tanh(r * k_tanh)
        u = xc * gi * jnp.where(z > 0.0, z * lax.rsqrt(z), 0.0)

        A, H = a.reshape(grouped), u.reshape(grouped)
        for d in (1, 2, 4):
            if reverse:
                valid, shift = sub < SUBLANES - d, SUBLANES - d
            else:
                valid, shift = sub >= d, d
            A_s = jnp.where(valid, pltpu.roll(A, shift, 1), 1.0)
            H_s = jnp.where(valid, pltpu.roll(H, shift, 1), 0.0)
            H = H + A * H_s
            A = A * A_s

        for g in backwards(n_groups):
            hg = H[g] + A[g] * c
            out_ref[lo + g * SUBLANES:lo + (g + 1) * SUBLANES, :] = hg
            c = hg[0:1, :] if reverse else hg[SUBLANES - 1:SUBLANES, :]
        yield
    carry_ref[...] = c


def _gla_pieces(qe_ref, ke_ref, kd_ref, dl_ref, v_ref, st_ref, out_ref, reverse):
    T = qe_ref.shape[0]
    C = GLA_CHUNK
    lane_k = lax.broadcasted_iota(jnp.int32, (C, GLA_DK_TOTAL), 1)
    lane_s = lax.broadcasted_iota(jnp.int32, (GLA_DV, GLA_DK_TOTAL), 1)
    ri = lax.broadcasted_iota(jnp.int32, (C, C), 0)
    ci = lax.broadcasted_iota(jnp.int32, (C, C), 1)
    causal = (ci >= ri) if reverse else (ci <= ri)

    n_chunks = T // C
    order = list(range(n_chunks - 1, -1, -1) if reverse else range(n_chunks))
    rows = [slice(c * C, (c + 1) * C) for c in range(n_chunks)]

    upd = {}
    for n_done, c in enumerate(order):
        if n_done and n_done % GLA_CHUNKS_PER_PIECE == 0:
            yield
        p = lax.dot_general(v_ref[rows[c], :], kd_ref[rows[c], :], (((0,), (0,)), ((), ())),
                            preferred_element_type=F32)
        u = p[:GLA_DV, :]
        for h in range(1, GLA_HEADS):
            u = jnp.where(lane_s >= h * GLA_DK, p[h * GLA_DV:(h + 1) * GLA_DV, :], u)
        upd[c] = u

    st = st_ref[...]
    st_in = {}
    for c in order:
        st_in[c] = st.astype(BF16)
        st = st * dl_ref[c:c + 1, :] + upd[c]
    st_ref[...] = st

    for n_done, c in enumerate(order):
        if n_done % GLA_CHUNKS_PER_PIECE == 0:
            yield
        rhs = jnp.concatenate([st_in[c], ke_ref[rows[c], :]], axis=0)
        qe = qe_ref[rows[c], :]
        lhs = jnp.concatenate(
            [jnp.where((lane_k >= h * GLA_DK) & (lane_k < (h + 1) * GLA_DK), qe,
                       jnp.zeros_like(qe)) for h in range(GLA_HEADS)], axis=0)
        so = lax.dot_general(lhs, rhs, (((1,), (1,)), ((), ())),
                             preferred_element_type=F32)
        for h in range(GLA_HEADS):
            soh = so[h * C:(h + 1) * C, :]
            scores = jnp.where(causal, soh[:, GLA_DV:], 0.0).astype(BF16)
            vh = v_ref[rows[c], h * GLA_DV:(h + 1) * GLA_DV]
            out_ref[rows[c], h * GLA_DV:(h + 1) * GLA_DV] = soh[:, :GLA_DV] + _dot(scores, vh)


def _mixer_kernel(xcf_ref, vf_ref, qef_ref, kef_ref, kdf_ref, dlf_ref,
                  xcb_ref, vb_ref, qeb_ref, keb_ref, kdb_ref, dlb_ref,
                  wa_ref, ba_ref, wx_ref, bx_ref, lam_ref,
                  hf_ref, hb_ref, of_ref, ob_ref,
                  cf_ref, cbk_ref, stf_ref, stb_ref):
    @pl.when(pl.program_id(0) == 0)
    def _():
        cf_ref[...] = jnp.zeros_like(cf_ref)
        cbk_ref[...] = jnp.zeros_like(cbk_ref)
        stf_ref[...] = jnp.zeros_like(stf_ref)
        stb_ref[...] = jnp.zeros_like(stb_ref)

    streams = [
        _gla_pieces(qef_ref, kef_ref, kdf_ref, dlf_ref, vf_ref, stf_ref, of_ref, False),
        _lru_pieces(xcf_ref, wa_ref[0], ba_ref[0:1, :], wx_ref[0], bx_ref[0:1, :],
                    lam_ref[0:1, :], False, cf_ref, hf_ref),
        _gla_pieces(qeb_ref, keb_ref, kdb_ref, dlb_ref, vb_ref, stb_ref, ob_ref, True),
        _lru_pieces(xcb_ref, wa_ref[1], ba_ref[1:2, :], wx_ref[1], bx_ref[1:2, :],
                    lam_ref[1:2, :], True, cbk_ref, hb_ref),
    ]
    _round_robin(streams)


def _mixer(xc, v, gla_f, gla_b, layer, wa, ba, wx, bx, lam):
    S = xc.shape[0]
    T = TILE_MIX
    n = S // T
    full = lambda a: _layer_spec(a, layer)

    def specs(rev):
        t = (lambda i: n - 1 - i) if rev else (lambda i: i)
        blk = lambda width: pl.BlockSpec((T, width), lambda i: (t(i), 0))
        dl = pl.BlockSpec((T // GLA_CHUNK, GLA_DK_TOTAL), lambda i: (t(i), 0))
        ins = [blk(D_RNN), blk(GLA_DV_TOTAL), blk(GLA_DK_TOTAL), blk(GLA_DK_TOTAL),
               blk(GLA_DK_TOTAL), dl]
        return ins, blk(D_RNN), blk(GLA_DV_TOTAL)

    ins_f, h_f, o_f = specs(False)
    ins_b, h_b, o_b = specs(True)
    params = (wa, ba, wx, bx, lam)
    return pl.pallas_call(
        _mixer_kernel,
        grid=(n,),
        in_specs=ins_f + ins_b + [full(p) for p in params],
        out_specs=[h_f, h_b, o_f, o_b],
        out_shape=(jax.ShapeDtypeStruct((S, D_RNN), F32),) * 2
                  + (jax.ShapeDtypeStruct((S, GLA_DV_TOTAL), F32),) * 2,
        scratch_shapes=[pltpu.VMEM((1, D_RNN), F32)] * 2
                       + [pltpu.VMEM((GLA_DV, GLA_DK_TOTAL), F32)] * 2,
        compiler_params=pltpu.CompilerParams(
            dimension_semantics=("arbitrary",), vmem_limit_bytes=VMEM_LIMIT_BYTES),
        name="mixer",
    )(xc, v, *gla_f, xc, v, *gla_b, *params)


def _out_ffn_pieces(rows, x_ref, hf_ref, hb_ref, gate_ref, of_ref, ob_ref, g_ref,
                    rnn_norm_ref, gla_norm_ref, wo_ref, mix_post_ref,
                    ffn_pre_ref, wg_ref, wu_ref, wd_ref, ffn_post_ref, y_ref):
    gla_gain = gla_norm_ref[...]
    parts = [_rms_norm((hf_ref[rows, :] + hb_ref[rows, :]) * gate_ref[rows, :],
                       rnn_norm_ref[...])]
    for h in range(GLA_HEADS):
        cols = slice(h * GLA_DV, (h + 1) * GLA_DV)
        parts.append(_rms_norm(of_ref[rows, cols] + ob_ref[rows, cols], gla_gain)
                     * g_ref[rows, cols])
    y = jnp.concatenate([p.astype(BF16) for p in parts], axis=-1)
    m = _dot(y, wo_ref[...])
    yield
    x1 = x_ref[rows, :] + _rms_norm(m, mix_post_ref[...])
    h2 = _rms_norm(x1, ffn_pre_ref[...]).astype(BF16)
    d_ff = wg_ref.shape[1]
    cols = [slice(c, min(c + FFN_CHUNK, d_ff)) for c in range(0, d_ff, FFN_CHUNK)]
    n_chunks = len(cols)
    gate_up = lambda j: (_dot(h2, wg_ref[:, cols[j]]), _dot(h2, wu_ref[:, cols[j]]))
    f = None
    a, u = gate_up(0)
    for j in range(n_chunks):
        yield
        nxt = gate_up(j + 1) if j + 1 < n_chunks else None
        z = (a * _sigmoid(a) * u).astype(BF16)
        fj = _dot(z, wd_ref[cols[j], :])
        f = fj if f is None else f + fj
        if nxt is not None:
            a, u = nxt
    yield
    y_ref[rows, :] = x1 + _rms_norm(f, ffn_post_ref[...])


def _out_ffn_kernel(*refs):
    T = refs[0].shape[0]
    streams = [_out_ffn_pieces(slice(r, r + OUT_ROWS), *refs) for r in range(0, T, OUT_ROWS)]
    _round_robin(streams)


def _out_ffn(x, hf, hb, gate, of, ob, g, layer, rnn_norm, gla_norm, wo, mix_post,
             ffn_pre, wg, wu, wd, ffn_post):
    S = x.shape[0]
    T = TILE_OUT
    row = lambda n: pl.BlockSpec((T, n), lambda i: (i, 0))
    const = lambda a: _layer_spec(a, layer, pipeline_mode=pl.Buffered(1))
    return pl.pallas_call(
        _out_ffn_kernel,
        grid=(S // T,),
        in_specs=[row(D_MODEL), row(D_RNN), row(D_RNN), row(D_RNN), row(GLA_DV_TOTAL),
                  row(GLA_DV_TOTAL), row(GLA_DV_TOTAL), const(rnn_norm), const(gla_norm),
                  const(wo), const(mix_post), const(ffn_pre),
                  const(wg), const(wu), const(wd), const(ffn_post)],
        out_specs=row(D_MODEL),
        out_shape=jax.ShapeDtypeStruct((S, D_MODEL), F32),
        compiler_params=pltpu.CompilerParams(
            dimension_semantics=("parallel",), vmem_limit_bytes=VMEM_LIMIT_BYTES),
        name="out_ffn",
    )(x, hf, hb, gate, of, ob, g, rnn_norm, gla_norm, wo, mix_post,
      ffn_pre, wg, wu, wd, ffn_post)


def _block_diag(w):
    eye = jnp.eye(RNN_HEADS, dtype=w.dtype)
    dense = jnp.einsum('ldhij,hg->ldhigj', w, eye)
    return dense.reshape(w.shape[0], w.shape[1], D_RNN, D_RNN).astype(BF16)


def _rows(v):
    return v.reshape(v.shape[0], 1, -1)


def kernel(x, mix_norm_pre, mix_norm_post, w_in, conv_w, conv_b, lru_w_a, lru_b_a, lru_w_x,
           lru_b_x, lru_lambda, rnn_out_norm, gla_w_gate, gla_b_gate, gla_out_norm, w_out,
           ffn_norm_pre, ffn_norm_post, w_ffn_gate, w_ffn_up, w_ffn_down):
    B, S, D = x.shape
    depth = w_in.shape[0]
    assert B == 1
    xs = x.reshape(B * S, D)

    w_main = w_in[:, :, :_P_END].astype(BF16)
    w_lr = jnp.pad(w_in[:, :, _P_END:], ((0, 0), (0, 0), (0, LANES - 2 * GLA_RANK))).astype(BF16)
    w_gate = jnp.zeros((depth, LANES, 2 * GLA_DK_TOTAL), F32)
    w_gate = w_gate.at[:, :GLA_RANK, :GLA_DK_TOTAL].set(gla_w_gate[:, 0])
    w_gate = w_gate.at[:, GLA_RANK:2 * GLA_RANK, GLA_DK_TOTAL:].set(gla_w_gate[:, 1])
    w_gate = w_gate.astype(BF16)
    b_gate = gla_b_gate.reshape(depth, 1, 2 * GLA_DK_TOTAL)
    wa, wx = _block_diag(lru_w_a), _block_diag(lru_w_x)
    wo, wg, wu, wd = (w.astype(BF16) for w in (w_out, w_ffn_gate, w_ffn_up, w_ffn_down))
    mix_pre, mix_post, ffn_pre, ffn_post, rnn_norm, gla_norm, cb = (
        _rows(v) for v in (mix_norm_pre, mix_norm_post, ffn_norm_pre, ffn_norm_post,
                           rnn_out_norm, gla_out_norm, conv_b))

    for l in range(depth):
        outs = _inproj(xs, l, mix_pre, w_main, w_lr, w_gate, b_gate, conv_w, cb)
        xc, gate, v, g = outs[:4]
        gla_f, gla_b = outs[4:8], outs[8:12]
        hf, hb, of, ob = _mixer(xc, v, gla_f, gla_b, l, wa, lru_b_a, wx, lru_b_x, lru_lambda)
        xs = _out_ffn(xs, hf, hb, gate, of, ob, g, l, rnn_norm, gla_norm, wo, mix_post,
                      ffn_pre, wg, wu, wd, ffn_post)
    return xs.reshape(B, S, D)
```

```python
import functools

import jax
import jax.numpy as jnp
from jax import lax
from jax.experimental import pallas as pl
from jax.experimental.pallas import tpu as pltpu

F32 = jnp.float32
BF16 = jnp.bfloat16

D_MODEL = 1024
D_RNN = 512
RNN_HEADS = 8
RNN_BLOCK = D_RNN // RNN_HEADS
CONV_WIDTH = 4
LRU_C = 8.0
GLA_HEADS = 4
GLA_DV_TOTAL = 512
GLA_DK_TOTAL = 256
GLA_DV = GLA_DV_TOTAL // GLA_HEADS
GLA_DK = GLA_DK_TOTAL // GLA_HEADS
GLA_RANK = 16
GLA_TAU = 16.0
GLA_CHUNK = 64
RMS_EPS = 1e-6

SUBLANES = 8
LANES = 128
HALO_ROWS = SUBLANES
VMEM_LIMIT_BYTES = 56 * 1024 * 1024

TILE_PROJ = 512
PROJ_ROWS = 128
PROJ_STAGGER = 0
TILE_MIX = 512
TILE_OUT = 512
OUT_ROWS = 256
FFN_CHUNK = 256
FFN_STAGE_DELAYS = (1, 1)
OUT_STAGGER = 0
LRU_BLOCK = 128
LRU_AHEAD = 1
LRU_SUB = 32
GLA_STAGE_DELAYS = (2, 4, 6, 8)
W_PREP_ROWS = 256
MIX_STAGGER = 0
_DONE = object()

_P_XR = 0
_P_GR = _P_XR + D_RNN
_P_Q = _P_GR + D_RNN
_P_K = _P_Q + GLA_DK_TOTAL
_P_V = _P_K + GLA_DK_TOTAL
_P_G = _P_V + GLA_DV_TOTAL
_P_END = _P_G + GLA_DV_TOTAL


def _rms_norm(x, gain):
    ms = jnp.mean(x * x, axis=-1, keepdims=True)
    return x * lax.rsqrt(ms + RMS_EPS) * gain


def _sigmoid(x):
    return 1.0 / (1.0 + jnp.exp(-x))


def _softplus(x):
    return jnp.maximum(x, 0.0) + jnp.log1p(jnp.exp(-jnp.abs(x)))


def _log_sigmoid(x):
    return jnp.minimum(x, 0.0) - jnp.log(1.0 + jnp.exp(-jnp.abs(x)))


def _gelu_tanh(x):
    c = 0.7978845608028654
    half = 0.5 * x
    return half + half * jnp.tanh(x * (c + (c * 0.044715) * (x * x)))


def _dot(a, b):
    return jnp.dot(a, b, preferred_element_type=F32)


def _layer_spec(a, layer, **kwargs):
    return pl.BlockSpec((None,) + a.shape[1:], lambda i: (layer,) + (0,) * (a.ndim - 1), **kwargs)


def _round_robin(streams, stagger=0):
    waiting = list(streams)
    active = []
    joined = rounds = 0
    while waiting or active:
        while waiting and joined * stagger <= rounds:
            active.append(waiting.pop(0))
            joined += 1
        for s in list(active):
            if next(s, _DONE) is _DONE:
                active.remove(s)
        rounds += 1


def _gla_operands(q, k, la, reverse, r0, qe_ref, ke_ref, kd_ref, dl_ref):
    R, C = la.shape[0], GLA_CHUNK
    rowc = jnp.bitwise_and(lax.broadcasted_iota(jnp.int32, la.shape, 0), C - 1)
    b = la
    d = 1
    while d < C:
        if reverse:
            valid, shift = rowc < C - d, R - d
        else:
            valid, shift = rowc >= d, d
        b = b + jnp.where(valid, pltpu.roll(b, shift, 0), 0.0)
        d *= 2
    for c in range(R // C):
        lo = c * C
        bc = b[lo:lo + C, :]
        bl = bc[0:1, :] if reverse else bc[C - 1:C, :]
        qe_ref[r0 + lo:r0 + lo + C, :] = (q[lo:lo + C, :] * jnp.exp(bc)).astype(BF16)
        ke_ref[r0 + lo:r0 + lo + C, :] = (k[lo:lo + C, :] * jnp.exp(-bc)).astype(BF16)
        kd_ref[r0 + lo:r0 + lo + C, :] = (k[lo:lo + C, :] * jnp.exp(bl - bc)).astype(BF16)
        ci = (r0 + lo) // C
        dl_ref[ci:ci + 1, :] = jnp.exp(bl)


def _inproj_pieces(r0, x_ref, xp_ref, xn_ref, gain_ref, w_ref, wlr_ref, wg_ref, bg_ref,
                   cw_ref, cb_ref, xc_ref, gate_ref, v_ref, g_ref,
                   qef_ref, kef_ref, kdf_ref, dlf_ref, qeb_ref, keb_ref, kdb_ref, dlb_ref):
    T, R = x_ref.shape[0], PROJ_ROWS
    rows = slice(r0, r0 + R)
    i, n = pl.program_id(0), pl.num_programs(0)
    if r0 == 0:
        x_prev, pad_prev = xp_ref[...], i == 0
    else:
        x_prev, pad_prev = x_ref[r0 - SUBLANES:r0, :], False
    if r0 + R == T:
        x_next, pad_next = xn_ref[...], i == n - 1
    else:
        x_next, pad_next = x_ref[r0 + R:r0 + R + SUBLANES, :], False
    x_ext = jnp.concatenate([x_ref[rows, :], x_prev, x_next], axis=0)
    h_ext = _rms_norm(x_ext, gain_ref[...]).astype(BF16)
    h = h_ext[:R, :]
    yield
    lr = _dot(h, wlr_ref[...]).astype(BF16)
    xr = _dot(h_ext, w_ref[:, _P_XR:_P_GR])
    yield
    z = _dot(lr, wg_ref[...]) + bg_ref[...]
    gate = _dot(h, w_ref[:, _P_GR:_P_Q])
    grouped = (R // SUBLANES, SUBLANES, xr.shape[1])
    sub = lax.broadcasted_iota(jnp.int32, grouped, 1)
    xg = xr[:R, :].reshape(grouped)
    prev8 = jnp.where(pad_prev, 0.0, xr[R:R + SUBLANES, :])[None]
    next8 = jnp.where(pad_next, 0.0, xr[R + SUBLANES:, :])[None]

    def shifted(k):
        rot = pltpu.roll(xg, (-k) % SUBLANES, 1)
        if k < 0:
            edge = pltpu.roll(prev8, (-k) % SUBLANES, 1)
            return jnp.where(sub >= -k, rot, jnp.concatenate([edge, rot[:-1]], axis=0))
        edge = pltpu.roll(next8, (-k) % SUBLANES, 1)
        return jnp.where(sub < SUBLANES - k, rot, jnp.concatenate([rot[1:], edge], axis=0))

    cw = cw_ref[...]
    xc_ref[rows, :] = (cw[0:1, :] * shifted(-2) + cw[1:2, :] * shifted(-1) + cw[2:3, :] * xg
                       + cw[3:4, :] * shifted(1) + cb_ref[...]).reshape(R, -1)
    yield
    q = _dot(h, w_ref[:, _P_Q:_P_K]) * (GLA_DK ** -0.5)
    k = _dot(h, w_ref[:, _P_K:_P_V])
    la = _log_sigmoid(z) * (1.0 / GLA_TAU)
    yield
    g = _dot(h, w_ref[:, _P_G:_P_END])
    gate_ref[rows, :] = _gelu_tanh(gate)
    _gla_operands(q, k, la[:, :GLA_DK_TOTAL], False, r0, qef_ref, kef_ref, kdf_ref, dlf_ref)
    yield
    v_ref[rows, :] = _dot(h, w_ref[:, _P_V:_P_G]).astype(BF16)
    g_ref[rows, :] = g * _sigmoid(g)
    _gla_operands(q, k, la[:, GLA_DK_TOTAL:], True, r0, qeb_ref, keb_ref, kdb_ref, dlb_ref)


def _inproj_kernel(*refs):
    T = refs[0].shape[0]
    _round_robin((_inproj_pieces(r0, *refs) for r0 in range(0, T, PROJ_ROWS)),
                 stagger=PROJ_STAGGER)


def _inproj(x, layer, gain, w_main, w_lr, w_gate, b_gate, conv_w, conv_b):
    S = x.shape[0]
    T = TILE_PROJ
    halo_per_tile = T // HALO_ROWS
    n_halo = S // HALO_ROWS
    row = lambda n: pl.BlockSpec((T, n), lambda i: (i, 0))
    full = lambda a: _layer_spec(a, layer)
    prev = pl.BlockSpec((HALO_ROWS, D_MODEL),
                        lambda i: (jnp.maximum(i * halo_per_tile - 1, 0), 0))
    nxt = pl.BlockSpec((HALO_ROWS, D_MODEL),
                       lambda i: (jnp.minimum((i + 1) * halo_per_tile, n_halo - 1), 0))
    dl_spec = pl.BlockSpec((T // GLA_CHUNK, GLA_DK_TOTAL), lambda i: (i, 0))
    gla_shapes = (
        jax.ShapeDtypeStruct((S, GLA_DK_TOTAL), BF16),
        jax.ShapeDtypeStruct((S, GLA_DK_TOTAL), BF16),
        jax.ShapeDtypeStruct((S, GLA_DK_TOTAL), BF16),
        jax.ShapeDtypeStruct((S // GLA_CHUNK, GLA_DK_TOTAL), F32),
    )
    gla_specs = [row(GLA_DK_TOTAL)] * 3 + [dl_spec]
    out_shapes = (
        jax.ShapeDtypeStruct((S, D_RNN), F32),
        jax.ShapeDtypeStruct((S, D_RNN), F32),
        jax.ShapeDtypeStruct((S, GLA_DV_TOTAL), BF16),
        jax.ShapeDtypeStruct((S, GLA_DV_TOTAL), F32),
    ) + gla_shapes + gla_shapes
    params = (gain, w_main, w_lr, w_gate, b_gate, conv_w, conv_b)
    return pl.pallas_call(
        _inproj_kernel,
        grid=(S // T,),
        in_specs=[row(D_MODEL), prev, nxt] + [full(p) for p in params],
        out_specs=[row(D_RNN), row(D_RNN), row(GLA_DV_TOTAL), row(GLA_DV_TOTAL)]
                  + gla_specs + gla_specs,
        out_shape=out_shapes,
        compiler_params=pltpu.CompilerParams(
            dimension_semantics=("parallel",), vmem_limit_bytes=VMEM_LIMIT_BYTES),
        name="inproj",
    )(x, x, x, *params)


def _lru_pieces(xc_ref, wa, ba, wx, bx, lam, reverse, carry_ref, out_ref):
    T = xc_ref.shape[0]
    R, Q = LRU_BLOCK, LRU_SUB
    n_blocks, n_sub, n_groups = T // R, R // Q, Q // SUBLANES
    grouped = (n_groups, SUBLANES, xc_ref.shape[1])
    sub = lax.broadcasted_iota(jnp.int32, grouped, 1)
    sp = _softplus(-lam)
    k_exp2 = (-LRU_C * 1.4426950408889634) * sp
    k_tanh = LRU_C * sp
    backwards = lambda n: list(range(n - 1, -1, -1) if reverse else range(n))
    blocks = backwards(n_blocks)

    def gate_matmuls(blk):
        xcb = xc_ref[blk * R:(blk + 1) * R, :].astype(BF16)
        return _dot(xcb, wa), _dot(xcb, wx)

    c = carry_ref[...]
    pre = [gate_matmuls(blk) for blk in blocks[:LRU_AHEAD]]
    yield
    for n_done, blk in enumerate(blocks):
        ya, yx = pre.pop(0)
        for s_done, s in enumerate(backwards(n_sub)):
            if s_done == 0 and n_done + LRU_AHEAD < n_blocks:
                pre.append(gate_matmuls(blocks[n_done + LRU_AHEAD]))
            q0 = blk * R + s * Q
            xc = xc_ref[q0:q0 + Q, :]
            r = _sigmoid(ya[s * Q:(s + 1) * Q, :] + ba)
            gi = _sigmoid(yx[s * Q:(s + 1) * Q, :] + bx)
            a = jnp.exp2(r * k_exp2)
            z = (1.0 + a * a) * jnp.tanh(r * k_tanh)
            u = xc * gi * jnp.where(z > 0.0, z * lax.rsqrt(z), 0.0)

            A, H = a.reshape(grouped), u.reshape(grouped)
            for d in (1, 2, 4):
                if reverse:
                    valid, shift = sub < SUBLANES - d, SUBLANES - d
                else:
                    valid, shift = sub >= d, d
                A_s = jnp.where(valid, pltpu.roll(A, shift, 1), 1.0)
                H_s = jnp.where(valid, pltpu.roll(H, shift, 1), 0.0)
                H = H + A * H_s
                A = A * A_s

            for g in backwards(n_groups):
                hg = H[g] + A[g] * c
                out_ref[q0 + g * SUBLANES:q0 + (g + 1) * SUBLANES, :] = hg
                c = hg[0:1, :] if reverse else hg[SUBLANES - 1:SUBLANES, :]
            yield
    carry_ref[...] = c


def _gla_pieces(qe_ref, ke_ref, kd_ref, dl_ref, v_ref, st_ref, out_ref, reverse):
    T = qe_ref.shape[0]
    C = GLA_CHUNK
    lane_k = lax.broadcasted_iota(jnp.int32, (C, GLA_DK_TOTAL), 1)
    lane_s = lax.broadcasted_iota(jnp.int32, (GLA_DV, GLA_DK_TOTAL), 1)
    ri = lax.broadcasted_iota(jnp.int32, (C, C), 0)
    ci = lax.broadcasted_iota(jnp.int32, (C, C), 1)
    causal = (ci >= ri) if reverse else (ci <= ri)

    n_chunks = T // C
    order = list(range(n_chunks - 1, -1, -1) if reverse else range(n_chunks))
    rows = [slice(c * C, (c + 1) * C) for c in range(n_chunks)]

    def increment(c):
        return lax.dot_general(v_ref[rows[c], :], kd_ref[rows[c], :], (((0,), (0,)), ((), ())),
                               preferred_element_type=F32)

    def diagonal(p):
        u = p[:GLA_DV, :]
        for h in range(1, GLA_HEADS):
            u = jnp.where(lane_s >= h * GLA_DK, p[h * GLA_DV:(h + 1) * GLA_DV, :], u)
        return u

    def inter_and_scores(c, st_in):
        rhs = jnp.concatenate([st_in, ke_ref[rows[c], :]], axis=0)
        qe = qe_ref[rows[c], :]
        lhs = jnp.concatenate(
            [jnp.where((lane_k >= h * GLA_DK) & (lane_k < (h + 1) * GLA_DK), qe,
                       jnp.zeros_like(qe)) for h in range(GLA_HEADS)], axis=0)
        return lax.dot_general(lhs, rhs, (((1,), (1,)), ((), ())),
                               preferred_element_type=F32)

    def intra(c, so):
        parts = []
        for h in range(GLA_HEADS):
            soh = so[h * C:(h + 1) * C, :]
            scores = jnp.where(causal, soh[:, GLA_DV:], 0.0).astype(BF16)
            parts.append((soh[:, :GLA_DV],
                          _dot(scores, v_ref[rows[c], h * GLA_DV:(h + 1) * GLA_DV])))
        return parts

    def store(c, parts):
        for h, (o_inter, o_intra) in enumerate(parts):
            out_ref[rows[c], h * GLA_DV:(h + 1) * GLA_DV] = o_inter + o_intra

    st = st_ref[...]
    p, st_in, so, parts = {}, {}, {}, {}
    d1, d2, d3, d4 = GLA_STAGE_DELAYS
    for t in range(n_chunks + d4):
        if t < n_chunks:
            p[t] = increment(order[t])
        if 0 <= t - d1 < n_chunks:
            c = order[t - d1]
            st_in[t - d1] = st.astype(BF16)
            st = st * dl_ref[c:c + 1, :] + diagonal(p.pop(t - d1))
        if 0 <= t - d2 < n_chunks:
            so[t - d2] = inter_and_scores(order[t - d2], st_in.pop(t - d2))
        if 0 <= t - d3 < n_chunks:
            parts[t - d3] = intra(order[t - d3], so.pop(t - d3))
        if 0 <= t - d4 < n_chunks:
            store(order[t - d4], parts.pop(t - d4))
        yield
    st_ref[...] = st


def _mixer_kernel(xcf_ref, vf_ref, qef_ref, kef_ref, kdf_ref, dlf_ref,
                  xcb_ref, vb_ref, qeb_ref, keb_ref, kdb_ref, dlb_ref,
                  wa_ref, ba_ref, wx_ref, bx_ref, lam_ref,
                  hf_ref, hb_ref, of_ref, ob_ref,
                  cf_ref, cbk_ref, stf_ref, stb_ref):
    @pl.when(pl.program_id(0) == 0)
    def _():
        cf_ref[...] = jnp.zeros_like(cf_ref)
        cbk_ref[...] = jnp.zeros_like(cbk_ref)
        stf_ref[...] = jnp.zeros_like(stf_ref)
        stb_ref[...] = jnp.zeros_like(stb_ref)

    streams = [
        _gla_pieces(qef_ref, kef_ref, kdf_ref, dlf_ref, vf_ref, stf_ref, of_ref, False),
        _lru_pieces(xcf_ref, wa_ref[0], ba_ref[0:1, :], wx_ref[0], bx_ref[0:1, :],
                    lam_ref[0:1, :], False, cf_ref, hf_ref),
        _gla_pieces(qeb_ref, keb_ref, kdb_ref, dlb_ref, vb_ref, stb_ref, ob_ref, True),
        _lru_pieces(xcb_ref, wa_ref[1], ba_ref[1:2, :], wx_ref[1], bx_ref[1:2, :],
                    lam_ref[1:2, :], True, cbk_ref, hb_ref),
    ]
    _round_robin(streams, stagger=MIX_STAGGER)


def _mixer(xc, v, gla_f, gla_b, layer, wa, ba, wx, bx, lam):
    S = xc.shape[0]
    T = TILE_MIX
    n = S // T
    full = lambda a: _layer_spec(a, layer)

    def specs(rev):
        t = (lambda i: n - 1 - i) if rev else (lambda i: i)
        blk = lambda width: pl.BlockSpec((T, width), lambda i: (t(i), 0))
        dl = pl.BlockSpec((T // GLA_CHUNK, GLA_DK_TOTAL), lambda i: (t(i), 0))
        ins = [blk(D_RNN), blk(GLA_DV_TOTAL), blk(GLA_DK_TOTAL), blk(GLA_DK_TOTAL),
               blk(GLA_DK_TOTAL), dl]
        return ins, blk(D_RNN), blk(GLA_DV_TOTAL)

    ins_f, h_f, o_f = specs(False)
    ins_b, h_b, o_b = specs(True)
    params = (wa, ba, wx, bx, lam)
    return pl.pallas_call(
        _mixer_kernel,
        grid=(n,),
        in_specs=ins_f + ins_b + [full(p) for p in params],
        out_specs=[h_f, h_b, o_f, o_b],
        out_shape=(jax.ShapeDtypeStruct((S, D_RNN), F32),) * 2
                  + (jax.ShapeDtypeStruct((S, GLA_DV_TOTAL), F32),) * 2,
        scratch_shapes=[pltpu.VMEM((1, D_RNN), F32)] * 2
                       + [pltpu.VMEM((GLA_DV, GLA_DK_TOTAL), F32)] * 2,
        compiler_params=pltpu.CompilerParams(
            dimension_semantics=("arbitrary",), vmem_limit_bytes=VMEM_LIMIT_BYTES),
        name="mixer",
    )(xc, v, *gla_f, xc, v, *gla_b, *params)


def _out_ffn_pieces(rows, x_ref, hf_ref, hb_ref, gate_ref, of_ref, ob_ref, g_ref,
                    rnn_norm_ref, gla_norm_ref, wo_ref, mix_post_ref,
                    ffn_pre_ref, wg_ref, wu_ref, wd_ref, ffn_post_ref, y_ref):
    gla_gain = gla_norm_ref[...]
    parts = [_rms_norm((hf_ref[rows, :] + hb_ref[rows, :]) * gate_ref[rows, :],
                       rnn_norm_ref[...])]
    for h in range(GLA_HEADS):
        cols = slice(h * GLA_DV, (h + 1) * GLA_DV)
        parts.append(_rms_norm(of_ref[rows, cols] + ob_ref[rows, cols], gla_gain)
                     * g_ref[rows, cols])
    y = jnp.concatenate([p.astype(BF16) for p in parts], axis=-1)
    m = _dot(y, wo_ref[...])
    yield
    x1 = x_ref[rows, :] + _rms_norm(m, mix_post_ref[...])
    h2 = _rms_norm(x1, ffn_pre_ref[...]).astype(BF16)
    d_ff = wg_ref.shape[1]
    cols = [slice(c, min(c + FFN_CHUNK, d_ff)) for c in range(0, d_ff, FFN_CHUNK)]
    n_chunks = len(cols)
    d1, d2 = FFN_STAGE_DELAYS
    au, z = {}, {}
    f = None
    for t in range(n_chunks + d2):
        if t < n_chunks:
            au[t] = (_dot(h2, wg_ref[:, cols[t]]), _dot(h2, wu_ref[:, cols[t]]))
        if 0 <= t - d1 < n_chunks:
            a, u = au.pop(t - d1)
            z[t - d1] = (a * _sigmoid(a) * u).astype(BF16)
        if 0 <= t - d2 < n_chunks:
            fj = _dot(z.pop(t - d2), wd_ref[cols[t - d2], :])
            f = fj if f is None else f + fj
        yield
    y_ref[rows, :] = x1 + _rms_norm(f, ffn_post_ref[...])


def _out_ffn_kernel(*refs):
    T = refs[0].shape[0]
    streams = [_out_ffn_pieces(slice(r, r + OUT_ROWS), *refs) for r in range(0, T, OUT_ROWS)]
    _round_robin(streams, stagger=OUT_STAGGER)


def _out_ffn(x, hf, hb, gate, of, ob, g, layer, rnn_norm, gla_norm, wo, mix_post,
             ffn_pre, wg, wu, wd, ffn_post):
    S = x.shape[0]
    T = TILE_OUT
    row = lambda n: pl.BlockSpec((T, n), lambda i: (i, 0))
    const = lambda a: _layer_spec(a, layer, pipeline_mode=pl.Buffered(1))
    return pl.pallas_call(
        _out_ffn_kernel,
        grid=(S // T,),
        in_specs=[row(D_MODEL), row(D_RNN), row(D_RNN), row(D_RNN), row(GLA_DV_TOTAL),
                  row(GLA_DV_TOTAL), row(GLA_DV_TOTAL), const(rnn_norm), const(gla_norm),
                  const(wo), const(mix_post), const(ffn_pre),
                  const(wg), const(wu), const(wd), const(ffn_post)],
        out_specs=row(D_MODEL),
        out_shape=jax.ShapeDtypeStruct((S, D_MODEL), F32),
        compiler_params=pltpu.CompilerParams(
            dimension_semantics=("parallel",), vmem_limit_bytes=VMEM_LIMIT_BYTES),
        name="out_ffn",
    )(x, hf, hb, gate, of, ob, g, rnn_norm, gla_norm, wo, mix_post,
      ffn_pre, wg, wu, wd, ffn_post)


def _split_w_in_kernel(w_ref, main_ref, lr_ref):
    main_ref[...] = w_ref[:, :_P_END].astype(BF16)
    lr_ref[...] = jnp.zeros_like(lr_ref)
    lr_ref[:, :2 * GLA_RANK] = w_ref[:, _P_END:].astype(BF16)


def _split_w_in(w_in):
    L, D, d_in = w_in.shape
    rows = W_PREP_ROWS
    blk = lambda n: pl.BlockSpec((None, rows, n), lambda l, r: (l, r, 0))
    return pl.pallas_call(
        _split_w_in_kernel,
        grid=(L, D // rows),
        in_specs=[blk(d_in)],
        out_specs=[blk(_P_END), blk(LANES)],
        out_shape=(jax.ShapeDtypeStruct((L, D, _P_END), BF16),
                   jax.ShapeDtypeStruct((L, D, LANES), BF16)),
        compiler_params=pltpu.CompilerParams(dimension_semantics=("parallel", "parallel")),
        name="split_w_in",
    )(w_in)
def _block_diag(w):
    eye = jnp.eye(RNN_HEADS, dtype=w.dtype)
    dense = jnp.einsum('ldhij,hg->ldhigj', w, eye)
    return dense.reshape(w.shape[0], w.shape[1], D_RNN, D_RNN).astype(BF16)


def _rows(v):
    return v.reshape(v.shape[0], 1, -1)


def kernel(x, mix_norm_pre, mix_norm_post, w_in, conv_w, conv_b, lru_w_a, lru_b_a, lru_w_x,
           lru_b_x, lru_lambda, rnn_out_norm, gla_w_gate, gla_b_gate, gla_out_norm, w_out,
           ffn_norm_pre, ffn_norm_post, w_ffn_gate, w_ffn_up, w_ffn_down):
    B, S, D = x.shape
    depth = w_in.shape[0]
    assert B == 1
    xs = x.reshape(B * S, D)

    w_main, w_lr = _split_w_in(w_in)
    w_gate = jnp.zeros((depth, LANES, 2 * GLA_DK_TOTAL), F32)
    w_gate = w_gate.at[:, :GLA_RANK, :GLA_DK_TOTAL].set(gla_w_gate[:, 0])
    w_gate = w_gate.at[:, GLA_RANK:2 * GLA_RANK, GLA_DK_TOTAL:].set(gla_w_gate[:, 1])
    w_gate = w_gate.astype(BF16)
    b_gate = gla_b_gate.reshape(depth, 1, 2 * GLA_DK_TOTAL)
    wa, wx = _block_diag(lru_w_a), _block_diag(lru_w_x)
    wo, wg, wu, wd = (w.astype(BF16) for w in (w_out, w_ffn_gate, w_ffn_up, w_ffn_down))
    mix_pre, mix_post, ffn_pre, ffn_post, rnn_norm, gla_norm, cb = (
        _rows(v) for v in (mix_norm_pre, mix_norm_post, ffn_norm_pre, ffn_norm_post,
                           rnn_out_norm, gla_out_norm, conv_b))

    for l in range(depth):
        outs = _inproj(xs, l, mix_pre, w_main, w_lr, w_gate, b_gate, conv_w, cb)
        xc, gate, v, g = outs[:4]
        gla_f, gla_b = outs[4:8], outs[8:12]
        hf, hb, of, ob = _mixer(xc, v, gla_f, gla_b, l, wa, lru_b_a, wx, lru_b_x, lru_lambda)
        xs = _out_ffn(xs, hf, hb, gate, of, ob, g, l, rnn_norm, gla_norm, wo, mix_post,
                      ffn_pre, wg, wu, wd, ffn_post)
    return xs.reshape(B, S, D)
```

```python
import functools

import jax
import jax.numpy as jnp
from jax import lax
from jax.experimental import pallas as pl
from jax.experimental.pallas import tpu as pltpu

F32 = jnp.float32
BF16 = jnp.bfloat16
ACT_DTYPE = BF16

D_MODEL = 1024
D_RNN = 512
RNN_HEADS = 8
RNN_BLOCK = D_RNN // RNN_HEADS
CONV_WIDTH = 4
LRU_C = 8.0
GLA_HEADS = 4
GLA_DV_TOTAL = 512
GLA_DK_TOTAL = 256
GLA_DV = GLA_DV_TOTAL // GLA_HEADS
GLA_DK = GLA_DK_TOTAL // GLA_HEADS
GLA_RANK = 16
GLA_TAU = 16.0
GLA_CHUNK = 64
RMS_EPS = 1e-6

SUBLANES = 8
LANES = 128
HALO_ROWS = SUBLANES
VMEM_LIMIT_BYTES = 56 * 1024 * 1024

TILE_PROJ = 1024
PROJ_ROWS = 128
PROJ_STAGGER = 0
TILE_MIX = 512
TILE_OUT = 512
OUT_ROWS = 256
FFN_CHUNK = 256
FFN_STAGE_DELAYS = (1, 1)
OUT_STAGGER = 0
LRU_BLOCK = 128
LRU_AHEAD = 1
LRU_SUB = 32
GLA_STAGE_DELAYS = (2, 4, 6, 8)
W_PREP_ROWS = 256
MIX_STAGGER = 0
_DONE = object()

_P_XR = 0
_P_GR = _P_XR + D_RNN
_P_Q = _P_GR + D_RNN
_P_K = _P_Q + GLA_DK_TOTAL
_P_V = _P_K + GLA_DK_TOTAL
_P_G = _P_V + GLA_DV_TOTAL
_P_END = _P_G + GLA_DV_TOTAL


def _rms_norm(x, gain):
    ms = jnp.mean(x * x, axis=-1, keepdims=True)
    return x * lax.rsqrt(ms + RMS_EPS) * gain


def _sigmoid(x):
    return 1.0 / (1.0 + jnp.exp(-x))


def _softplus(x):
    return jnp.maximum(x, 0.0) + jnp.log1p(jnp.exp(-jnp.abs(x)))


def _log_sigmoid(x):
    return jnp.minimum(x, 0.0) - jnp.log(1.0 + jnp.exp(-jnp.abs(x)))


def _gelu_tanh(x):
    c = 0.7978845608028654
    half = 0.5 * x
    return half + half * jnp.tanh(x * (c + (c * 0.044715) * (x * x)))


def _dot(a, b):
    return jnp.dot(a, b, preferred_element_type=F32)


def _layer_spec(a, layer, **kwargs):
    return pl.BlockSpec((None,) + a.shape[1:], lambda i: (layer,) + (0,) * (a.ndim - 1), **kwargs)


def _round_robin(streams, stagger=0):
    waiting = list(streams)
    active = []
    joined = rounds = 0
    while waiting or active:
        while waiting and joined * stagger <= rounds:
            active.append(waiting.pop(0))
            joined += 1
        for s in list(active):
            if next(s, _DONE) is _DONE:
                active.remove(s)
        rounds += 1


def _gla_operands(q, k, la, reverse, r0, qe_ref, ke_ref, kd_ref, dl_ref):
    R, C = la.shape[0], GLA_CHUNK
    rowc = jnp.bitwise_and(lax.broadcasted_iota(jnp.int32, la.shape, 0), C - 1)
    b = la
    d = 1
    while d < C:
        if reverse:
            valid, shift = rowc < C - d, R - d
        else:
            valid, shift = rowc >= d, d
        b = b + jnp.where(valid, pltpu.roll(b, shift, 0), 0.0)
        d *= 2
    for c in range(R // C):
        lo = c * C
        bc = b[lo:lo + C, :]
        bl = bc[0:1, :] if reverse else bc[C - 1:C, :]
        qe_ref[r0 + lo:r0 + lo + C, :] = (q[lo:lo + C, :] * jnp.exp(bc)).astype(BF16)
        ke_ref[r0 + lo:r0 + lo + C, :] = (k[lo:lo + C, :] * jnp.exp(-bc)).astype(BF16)
        kd_ref[r0 + lo:r0 + lo + C, :] = (k[lo:lo + C, :] * jnp.exp(bl - bc)).astype(BF16)
        ci = (r0 + lo) // C
        dl_ref[ci:ci + 1, :] = jnp.exp(bl)


def _inproj_pieces(r0, x_ref, xp_ref, xn_ref, gain_ref, w_ref, wlr_ref, wg_ref, bg_ref,
                   cw_ref, cb_ref, xc_ref, gate_ref, v_ref, g_ref,
                   qef_ref, kef_ref, kdf_ref, dlf_ref, qeb_ref, keb_ref, kdb_ref, dlb_ref):
    T, R = x_ref.shape[0], PROJ_ROWS
    rows = slice(r0, r0 + R)
    i, n = pl.program_id(0), pl.num_programs(0)
    if r0 == 0:
        x_prev, pad_prev = xp_ref[...], i == 0
    else:
        x_prev, pad_prev = x_ref[r0 - SUBLANES:r0, :], False
    if r0 + R == T:
        x_next, pad_next = xn_ref[...], i == n - 1
    else:
        x_next, pad_next = x_ref[r0 + R:r0 + R + SUBLANES, :], False
    x_ext = jnp.concatenate([x_ref[rows, :], x_prev, x_next], axis=0)
    h_ext = _rms_norm(x_ext, gain_ref[...]).astype(BF16)
    h = h_ext[:R, :]
    yield
    lr = _dot(h, wlr_ref[...]).astype(BF16)
    xr = _dot(h_ext, w_ref[:, _P_XR:_P_GR])
    yield
    z = _dot(lr, wg_ref[...]) + bg_ref[...]
    gate = _dot(h, w_ref[:, _P_GR:_P_Q])
    grouped = (R // SUBLANES, SUBLANES, xr.shape[1])
    sub = lax.broadcasted_iota(jnp.int32, grouped, 1)
    xg = xr[:R, :].reshape(grouped)
    prev8 = jnp.where(pad_prev, 0.0, xr[R:R + SUBLANES, :])[None]
    next8 = jnp.where(pad_next, 0.0, xr[R + SUBLANES:, :])[None]

    def shifted(k):
        rot = pltpu.roll(xg, (-k) % SUBLANES, 1)
        if k < 0:
            edge = pltpu.roll(prev8, (-k) % SUBLANES, 1)
            return jnp.where(sub >= -k, rot, jnp.concatenate([edge, rot[:-1]], axis=0))
        edge = pltpu.roll(next8, (-k) % SUBLANES, 1)
        return jnp.where(sub < SUBLANES - k, rot, jnp.concatenate([rot[1:], edge], axis=0))

    cw = cw_ref[...]
    xc_ref[rows, :] = (cw[0:1, :] * shifted(-2) + cw[1:2, :] * shifted(-1) + cw[2:3, :] * xg
                       + cw[3:4, :] * shifted(1) + cb_ref[...]).reshape(R, -1)
    yield
    q = _dot(h, w_ref[:, _P_Q:_P_K]) * (GLA_DK ** -0.5)
    k = _dot(h, w_ref[:, _P_K:_P_V])
    la = _log_sigmoid(z) * (1.0 / GLA_TAU)
    yield
    g = _dot(h, w_ref[:, _P_G:_P_END])
    gate_ref[rows, :] = _gelu_tanh(gate).astype(gate_ref.dtype)
    _gla_operands(q, k, la[:, :GLA_DK_TOTAL], False, r0, qef_ref, kef_ref, kdf_ref, dlf_ref)
    yield
    v_ref[rows, :] = _dot(h, w_ref[:, _P_V:_P_G]).astype(BF16)
    g_ref[rows, :] = (g * _sigmoid(g)).astype(g_ref.dtype)
    _gla_operands(q, k, la[:, GLA_DK_TOTAL:], True, r0, qeb_ref, keb_ref, kdb_ref, dlb_ref)


def _inproj_kernel(*refs):
    T = refs[0].shape[0]
    _round_robin((_inproj_pieces(r0, *refs) for r0 in range(0, T, PROJ_ROWS)),
                 stagger=PROJ_STAGGER)


def _inproj(x, layer, gain, w_main, w_lr, w_gate, b_gate, conv_w, conv_b):
    S = x.shape[0]
    T = TILE_PROJ
    halo_per_tile = T // HALO_ROWS
    n_halo = S // HALO_ROWS
    row = lambda n: pl.BlockSpec((T, n), lambda i: (i, 0))
    full = lambda a: _layer_spec(a, layer)
    prev = pl.BlockSpec((HALO_ROWS, D_MODEL),
                        lambda i: (jnp.maximum(i * halo_per_tile - 1, 0), 0))
    nxt = pl.BlockSpec((HALO_ROWS, D_MODEL),
                       lambda i: (jnp.minimum((i + 1) * halo_per_tile, n_halo - 1), 0))
    dl_spec = pl.BlockSpec((T // GLA_CHUNK, GLA_DK_TOTAL), lambda i: (i, 0))
    gla_shapes = (
        jax.ShapeDtypeStruct((S, GLA_DK_TOTAL), BF16),
        jax.ShapeDtypeStruct((S, GLA_DK_TOTAL), BF16),
        jax.ShapeDtypeStruct((S, GLA_DK_TOTAL), BF16),
        jax.ShapeDtypeStruct((S // GLA_CHUNK, GLA_DK_TOTAL), F32),
    )
    gla_specs = [row(GLA_DK_TOTAL)] * 3 + [dl_spec]
    out_shapes = (
        jax.ShapeDtypeStruct((S, D_RNN), F32),
        jax.ShapeDtypeStruct((S, D_RNN), ACT_DTYPE),
        jax.ShapeDtypeStruct((S, GLA_DV_TOTAL), BF16),
        jax.ShapeDtypeStruct((S, GLA_DV_TOTAL), ACT_DTYPE),
    ) + gla_shapes + gla_shapes
    params = (gain, w_main, w_lr, w_gate, b_gate, conv_w, conv_b)
    return pl.pallas_call(
        _inproj_kernel,
        grid=(S // T,),
        in_specs=[row(D_MODEL), prev, nxt] + [full(p) for p in params],
        out_specs=[row(D_RNN), row(D_RNN), row(GLA_DV_TOTAL), row(GLA_DV_TOTAL)]
                  + gla_specs + gla_specs,
        out_shape=out_shapes,
        compiler_params=pltpu.CompilerParams(
            dimension_semantics=("parallel",), vmem_limit_bytes=VMEM_LIMIT_BYTES),
        name="inproj",
    )(x, x, x, *params)


def _lru_pieces(xc_ref, wa, ba, wx, bx, lam, reverse, carry_ref, out_ref):
    T = xc_ref.shape[0]
    R, Q = LRU_BLOCK, LRU_SUB
    n_blocks, n_sub, n_groups = T // R, R // Q, Q // SUBLANES
    grouped = (n_groups, SUBLANES, xc_ref.shape[1])
    sub = lax.broadcasted_iota(jnp.int32, grouped, 1)
    sp = _softplus(-lam)
    k_exp2 = (-LRU_C * 1.4426950408889634) * sp
    k_tanh = LRU_C * sp
    backwards = lambda n: list(range(n - 1, -1, -1) if reverse else range(n))
    blocks = backwards(n_blocks)

    def gate_matmuls(blk):
        xcb = xc_ref[blk * R:(blk + 1) * R, :].astype(BF16)
        return _dot(xcb, wa), _dot(xcb, wx)

    c = carry_ref[...]
    pre = [gate_matmuls(blk) for blk in blocks[:LRU_AHEAD]]
    yield
    for n_done, blk in enumerate(blocks):
        ya, yx = pre.pop(0)
        for s_done, s in enumerate(backwards(n_sub)):
            if s_done == 0 and n_done + LRU_AHEAD < n_blocks:
                pre.append(gate_matmuls(blocks[n_done + LRU_AHEAD]))
            q0 = blk * R + s * Q
            xc = xc_ref[q0:q0 + Q, :]
            r = _sigmoid(ya[s * Q:(s + 1) * Q, :] + ba)
            gi = _sigmoid(yx[s * Q:(s + 1) * Q, :] + bx)
            a = jnp.exp2(r * k_exp2)
            z = (1.0 + a * a) * jnp.tanh(r * k_tanh)
            u = xc * gi * jnp.where(z > 0.0, z * lax.rsqrt(z), 0.0)

            A, H = a.reshape(grouped), u.reshape(grouped)
            for d in (1, 2, 4):
                if reverse:
                    valid, shift = sub < SUBLANES - d, SUBLANES - d
                else:
                    valid, shift = sub >= d, d
                A_s = jnp.where(valid, pltpu.roll(A, shift, 1), 1.0)
                H_s = jnp.where(valid, pltpu.roll(H, shift, 1), 0.0)
                H = H + A * H_s
                A = A * A_s

            hs = [None] * n_groups
            for g in backwards(n_groups):
                hs[g] = H[g] + A[g] * c
                c = hs[g][0:1, :] if reverse else hs[g][SUBLANES - 1:SUBLANES, :]
            out_ref[q0:q0 + Q, :] = jnp.concatenate(hs, axis=0).astype(out_ref.dtype)
            yield
    carry_ref[...] = c


def _gla_pieces(qe_ref, ke_ref, kd_ref, dl_ref, v_ref, st_ref, out_ref, reverse):
    T = qe_ref.shape[0]
    C = GLA_CHUNK
    lane_k = lax.broadcasted_iota(jnp.int32, (C, GLA_DK_TOTAL), 1)
    lane_s = lax.broadcasted_iota(jnp.int32, (GLA_DV, GLA_DK_TOTAL), 1)
    ri = lax.broadcasted_iota(jnp.int32, (C, C), 0)
    ci = lax.broadcasted_iota(jnp.int32, (C, C), 1)
    causal = (ci >= ri) if reverse else (ci <= ri)

    n_chunks = T // C
    order = list(range(n_chunks - 1, -1, -1) if reverse else range(n_chunks))
    rows = [slice(c * C, (c + 1) * C) for c in range(n_chunks)]

    def increment(c):
        return lax.dot_general(v_ref[rows[c], :], kd_ref[rows[c], :], (((0,), (0,)), ((), ())),
                               preferred_element_type=F32)

    def diagonal(p):
        u = p[:GLA_DV, :]
        for h in range(1, GLA_HEADS):
            u = jnp.where(lane_s >= h * GLA_DK, p[h * GLA_DV:(h + 1) * GLA_DV, :], u)
        return u

    def inter_and_scores(c, st_in):
        rhs = jnp.concatenate([st_in, ke_ref[rows[c], :]], axis=0)
        qe = qe_ref[rows[c], :]
        lhs = jnp.concatenate(
            [jnp.where((lane_k >= h * GLA_DK) & (lane_k < (h + 1) * GLA_DK), qe,
                       jnp.zeros_like(qe)) for h in range(GLA_HEADS)], axis=0)
        return lax.dot_general(lhs, rhs, (((1,), (1,)), ((), ())),
                               preferred_element_type=F32)

    def intra(c, so):
        parts = []
        for h in range(GLA_HEADS):
            soh = so[h * C:(h + 1) * C, :]
            scores = jnp.where(causal, soh[:, GLA_DV:], 0.0).astype(BF16)
            parts.append((soh[:, :GLA_DV],
                          _dot(scores, v_ref[rows[c], h * GLA_DV:(h + 1) * GLA_DV])))
        return parts

    def store(c, parts):
        for h, (o_inter, o_intra) in enumerate(parts):
            out_ref[rows[c], h * GLA_DV:(h + 1) * GLA_DV] = (o_inter + o_intra).astype(out_ref.dtype)

    st = st_ref[...]
    p, st_in, so, parts = {}, {}, {}, {}
    d1, d2, d3, d4 = GLA_STAGE_DELAYS
    for t in range(n_chunks + d4):
        if t < n_chunks:
            p[t] = increment(order[t])
        if 0 <= t - d1 < n_chunks:
            c = order[t - d1]
            st_in[t - d1] = st.astype(BF16)
            st = st * dl_ref[c:c + 1, :] + diagonal(p.pop(t - d1))
        if 0 <= t - d2 < n_chunks:
            so[t - d2] = inter_and_scores(order[t - d2], st_in.pop(t - d2))
        if 0 <= t - d3 < n_chunks:
            parts[t - d3] = intra(order[t - d3], so.pop(t - d3))
        if 0 <= t - d4 < n_chunks:
            store(order[t - d4], parts.pop(t - d4))
        yield
    st_ref[...] = st


def _mixer_kernel(xcf_ref, vf_ref, qef_ref, kef_ref, kdf_ref, dlf_ref,
                  xcb_ref, vb_ref, qeb_ref, keb_ref, kdb_ref, dlb_ref,
                  wa_ref, ba_ref, wx_ref, bx_ref, lam_ref,
                  hf_ref, hb_ref, of_ref, ob_ref,
                  cf_ref, cbk_ref, stf_ref, stb_ref):
    @pl.when(pl.program_id(0) == 0)
    def _():
        cf_ref[...] = jnp.zeros_like(cf_ref)
        cbk_ref[...] = jnp.zeros_like(cbk_ref)
        stf_ref[...] = jnp.zeros_like(stf_ref)
        stb_ref[...] = jnp.zeros_like(stb_ref)

    streams = [
        _gla_pieces(qef_ref, kef_ref, kdf_ref, dlf_ref, vf_ref, stf_ref, of_ref, False),
        _lru_pieces(xcf_ref, wa_ref[0], ba_ref[0:1, :], wx_ref[0], bx_ref[0:1, :],
                    lam_ref[0:1, :], False, cf_ref, hf_ref),
        _gla_pieces(qeb_ref, keb_ref, kdb_ref, dlb_ref, vb_ref, stb_ref, ob_ref, True),
        _lru_pieces(xcb_ref, wa_ref[1], ba_ref[1:2, :], wx_ref[1], bx_ref[1:2, :],
                    lam_ref[1:2, :], True, cbk_ref, hb_ref),
    ]
    _round_robin(streams, stagger=MIX_STAGGER)


def _mixer(xc, v, gla_f, gla_b, layer, wa, ba, wx, bx, lam):
    S = xc.shape[0]
    T = TILE_MIX
    n = S // T
    full = lambda a: _layer_spec(a, layer)

    def specs(rev):
        t = (lambda i: n - 1 - i) if rev else (lambda i: i)
        blk = lambda width: pl.BlockSpec((T, width), lambda i: (t(i), 0))
        dl = pl.BlockSpec((T // GLA_CHUNK, GLA_DK_TOTAL), lambda i: (t(i), 0))
        ins = [blk(D_RNN), blk(GLA_DV_TOTAL), blk(GLA_DK_TOTAL), blk(GLA_DK_TOTAL),
               blk(GLA_DK_TOTAL), dl]
        return ins, blk(D_RNN), blk(GLA_DV_TOTAL)

    ins_f, h_f, o_f = specs(False)
    ins_b, h_b, o_b = specs(True)
    params = (wa, ba, wx, bx, lam)
    return pl.pallas_call(
        _mixer_kernel,
        grid=(n,),
        in_specs=ins_f + ins_b + [full(p) for p in params],
        out_specs=[h_f, h_b, o_f, o_b],
        out_shape=(jax.ShapeDtypeStruct((S, D_RNN), ACT_DTYPE),) * 2
                  + (jax.ShapeDtypeStruct((S, GLA_DV_TOTAL), ACT_DTYPE),) * 2,
        scratch_shapes=[pltpu.VMEM((1, D_RNN), F32)] * 2
                       + [pltpu.VMEM((GLA_DV, GLA_DK_TOTAL), F32)] * 2,
        compiler_params=pltpu.CompilerParams(
            dimension_semantics=("arbitrary",), vmem_limit_bytes=VMEM_LIMIT_BYTES),
        name="mixer",
    )(xc, v, *gla_f, xc, v, *gla_b, *params)


def _out_ffn_pieces(rows, x_ref, hf_ref, hb_ref, gate_ref, of_ref, ob_ref, g_ref,
                    rnn_norm_ref, gla_norm_ref, wo_ref, mix_post_ref,
                    ffn_pre_ref, wg_ref, wu_ref, wd_ref, ffn_post_ref, y_ref):
    gla_gain = gla_norm_ref[...]
    f32 = lambda ref, cols=slice(None): ref[rows, cols].astype(F32)
    parts = [_rms_norm((f32(hf_ref) + f32(hb_ref)) * f32(gate_ref), rnn_norm_ref[...])]
    for h in range(GLA_HEADS):
        cols = slice(h * GLA_DV, (h + 1) * GLA_DV)
        parts.append(_rms_norm(f32(of_ref, cols) + f32(ob_ref, cols), gla_gain)
                     * f32(g_ref, cols))
    y = jnp.concatenate([p.astype(BF16) for p in parts], axis=-1)
    m = _dot(y, wo_ref[...])
    yield
    x1 = x_ref[rows, :] + _rms_norm(m, mix_post_ref[...])
    h2 = _rms_norm(x1, ffn_pre_ref[...]).astype(BF16)
    d_ff = wg_ref.shape[1]
    cols = [slice(c, min(c + FFN_CHUNK, d_ff)) for c in range(0, d_ff, FFN_CHUNK)]
    n_chunks = len(cols)
    d1, d2 = FFN_STAGE_DELAYS
    au, z = {}, {}
    f = None
    for t in range(n_chunks + d2):
        if t < n_chunks:
            au[t] = (_dot(h2, wg_ref[:, cols[t]]), _dot(h2, wu_ref[:, cols[t]]))
        if 0 <= t - d1 < n_chunks:
            a, u = au.pop(t - d1)
            z[t - d1] = (a * _sigmoid(a) * u).astype(BF16)
        if 0 <= t - d2 < n_chunks:
            fj = _dot(z.pop(t - d2), wd_ref[cols[t - d2], :])
            f = fj if f is None else f + fj
        yield
    y_ref[rows, :] = x1 + _rms_norm(f, ffn_post_ref[...])


def _out_ffn_kernel(*refs):
    T = refs[0].shape[0]
    streams = [_out_ffn_pieces(slice(r, r + OUT_ROWS), *refs) for r in range(0, T, OUT_ROWS)]
    _round_robin(streams, stagger=OUT_STAGGER)


def _out_ffn(x, hf, hb, gate, of, ob, g, layer, rnn_norm, gla_norm, wo, mix_post,
             ffn_pre, wg, wu, wd, ffn_post):
    S = x.shape[0]
    T = TILE_OUT
    row = lambda n: pl.BlockSpec((T, n), lambda i: (i, 0))
    const = lambda a: _layer_spec(a, layer, pipeline_mode=pl.Buffered(1))
    return pl.pallas_call(
        _out_ffn_kernel,
        grid=(S // T,),
        in_specs=[row(D_MODEL), row(D_RNN), row(D_RNN), row(D_RNN), row(GLA_DV_TOTAL),
                  row(GLA_DV_TOTAL), row(GLA_DV_TOTAL), const(rnn_norm), const(gla_norm),
                  const(wo), const(mix_post), const(ffn_pre),
                  const(wg), const(wu), const(wd), const(ffn_post)],
        out_specs=row(D_MODEL),
        out_shape=jax.ShapeDtypeStruct((S, D_MODEL), F32),
        compiler_params=pltpu.CompilerParams(
            dimension_semantics=("parallel",), vmem_limit_bytes=VMEM_LIMIT_BYTES),
        name="out_ffn",
    )(x, hf, hb, gate, of, ob, g, rnn_norm, gla_norm, wo, mix_post,
      ffn_pre, wg, wu, wd, ffn_post)


def _split_w_in_kernel(w_ref, main_ref, lr_ref):
    main_ref[...] = w_ref[:, :_P_END].astype(BF16)
    lr_ref[...] = jnp.zeros_like(lr_ref)
    lr_ref[:, :2 * GLA_RANK] = w_ref[:, _P_END:].astype(BF16)


def _split_w_in(w_in):
    L, D, d_in = w_in.shape
    rows = W_PREP_ROWS
    blk = lambda n: pl.BlockSpec((None, rows, n), lambda l, r: (l, r, 0))
    return pl.pallas_call(
        _split_w_in_kernel,
        grid=(L, D // rows),
        in_specs=[blk(d_in)],
        out_specs=[blk(_P_END), blk(LANES)],
        out_shape=(jax.ShapeDtypeStruct((L, D, _P_END), BF16),
                   jax.ShapeDtypeStruct((L, D, LANES), BF16)),
        compiler_params=pltpu.CompilerParams(dimension_semantics=("parallel", "parallel")),
        name="split_w_in",
    )(w_in)
def _block_diag(w):
    eye = jnp.eye(RNN_HEADS, dtype=w.dtype)
    dense = jnp.einsum('ldhij,hg->ldhigj', w, eye)
    return dense.reshape(w.shape[0], w.shape[1], D_RNN, D_RNN).astype(BF16)


def _rows(v):
    return v.reshape(v.shape[0], 1, -1)


def kernel(x, mix_norm_pre, mix_norm_post, w_in, conv_w, conv_b, lru_w_a, lru_b_a, lru_w_x,
           lru_b_x, lru_lambda, rnn_out_norm, gla_w_gate, gla_b_gate, gla_out_norm, w_out,
           ffn_norm_pre, ffn_norm_post, w_ffn_gate, w_ffn_up, w_ffn_down):
    B, S, D = x.shape
    depth = w_in.shape[0]
    assert B == 1
    xs = x.reshape(B * S, D)

    w_main, w_lr = _split_w_in(w_in.astype(BF16))
    w_gate = jnp.zeros((depth, LANES, 2 * GLA_DK_TOTAL), F32)
    w_gate = w_gate.at[:, :GLA_RANK, :GLA_DK_TOTAL].set(gla_w_gate[:, 0])
    w_gate = w_gate.at[:, GLA_RANK:2 * GLA_RANK, GLA_DK_TOTAL:].set(gla_w_gate[:, 1])
    w_gate = w_gate.astype(BF16)
    b_gate = gla_b_gate.reshape(depth, 1, 2 * GLA_DK_TOTAL)
    wa, wx = _block_diag(lru_w_a), _block_diag(lru_w_x)
    wo, wg, wu, wd = (w.astype(BF16) for w in (w_out, w_ffn_gate, w_ffn_up, w_ffn_down))
    mix_pre, mix_post, ffn_pre, ffn_post, rnn_norm, gla_norm, cb = (
        _rows(v) for v in (mix_norm_pre, mix_norm_post, ffn_norm_pre, ffn_norm_post,
                           rnn_out_norm, gla_out_norm, conv_b))

    for l in range(depth):
        outs = _inproj(xs, l, mix_pre, w_main, w_lr, w_gate, b_gate, conv_w, cb)
        xc, gate, v, g = outs[:4]
        gla_f, gla_b = outs[4:8], outs[8:12]
        hf, hb, of, ob = _mixer(xc, v, gla_f, gla_b, l, wa, lru_b_a, wx, lru_b_x, lru_lambda)
        xs = _out_ffn(xs, hf, hb, gate, of, ob, g, l, rnn_norm, gla_norm, wo, mix_post,
                      ffn_pre, wg, wu, wd, ffn_post)
    return xs.reshape(B, S, D)
```

```python
import functools

import jax
import jax.numpy as jnp
from jax import lax
from jax.experimental import pallas as pl
from jax.experimental.pallas import tpu as pltpu

F32 = jnp.float32
BF16 = jnp.bfloat16
ACT_DTYPE = F32

D_MODEL = 1024
D_RNN = 512
RNN_HEADS = 8
RNN_BLOCK = D_RNN // RNN_HEADS
CONV_WIDTH = 4
LRU_C = 8.0
GLA_HEADS = 4
GLA_DV_TOTAL = 512
GLA_DK_TOTAL = 256
GLA_DV = GLA_DV_TOTAL // GLA_HEADS
GLA_DK = GLA_DK_TOTAL // GLA_HEADS
GLA_RANK = 16
GLA_TAU = 16.0
GLA_CHUNK = 64
RMS_EPS = 1e-6

SUBLANES = 8
LANES = 128
HALO_ROWS = SUBLANES
VMEM_LIMIT_BYTES = 56 * 1024 * 1024

TILE_PROJ = 1024
PROJ_ROWS = 128
PROJ_STAGGER = 0
TILE_MIX = 512
TILE_OUT = 512
OUT_ROWS = 256
FFN_CHUNK = 256
FFN_STAGE_DELAYS = (1, 1)
OUT_STAGGER = 0
LRU_TILE = 512
LRU_SEG = LRU_TILE // SUBLANES
LRU_BLOCK = 128
LRU_AHEAD = 1
LRU_SUB = 32
GLA_STAGE_DELAYS = (2, 4, 6, 8)
W_PREP_ROWS = 256
MIX_STAGGER = 0
_DONE = object()

_P_XR = 0
_P_GR = _P_XR + D_RNN
_P_Q = _P_GR + D_RNN
_P_K = _P_Q + GLA_DK_TOTAL
_P_V = _P_K + GLA_DK_TOTAL
_P_G = _P_V + GLA_DV_TOTAL
_P_END = _P_G + GLA_DV_TOTAL


def _rms_norm(x, gain):
    ms = jnp.mean(x * x, axis=-1, keepdims=True)
    return x * lax.rsqrt(ms + RMS_EPS) * gain


def _sigmoid(x):
    return 1.0 / (1.0 + jnp.exp(-x))


def _softplus(x):
    return jnp.maximum(x, 0.0) + jnp.log1p(jnp.exp(-jnp.abs(x)))


def _log_sigmoid(x):
    return jnp.minimum(x, 0.0) - jnp.log(1.0 + jnp.exp(-jnp.abs(x)))


def _gelu_tanh(x):
    c = 0.7978845608028654
    half = 0.5 * x
    return half + half * jnp.tanh(x * (c + (c * 0.044715) * (x * x)))


def _dot(a, b):
    return jnp.dot(a, b, preferred_element_type=F32)


def _layer_spec(a, layer, **kwargs):
    return pl.BlockSpec((None,) + a.shape[1:], lambda i: (layer,) + (0,) * (a.ndim - 1), **kwargs)


def _scan_layout_shape(S):
    return (S // LRU_TILE, D_RNN // LANES, LRU_TILE, LANES)


def _scan_layout_spec(tile_rows, index):
    return pl.BlockSpec((tile_rows // LRU_TILE, D_RNN // LANES, LRU_TILE, LANES),
                        lambda i: (index(i), 0, 0, 0))


def _store_segment(ref, row0, vals):
    tile, seg = row0 // LRU_TILE, (row0 % LRU_TILE) // LRU_SEG
    for s in range(D_RNN // LANES):
        ref[tile, s, pl.ds(seg, LRU_SEG, stride=SUBLANES), :] = vals[:, s * LANES:(s + 1) * LANES]


def _load_segment(ref, row0):
    tile, seg = row0 // LRU_TILE, (row0 % LRU_TILE) // LRU_SEG
    return jnp.concatenate([ref[tile, s, pl.ds(seg, LRU_SEG, stride=SUBLANES), :]
                            for s in range(D_RNN // LANES)], axis=-1)


def _round_robin(streams, stagger=0):
    waiting = list(streams)
    active = []
    joined = rounds = 0
    while waiting or active:
        while waiting and joined * stagger <= rounds:
            active.append(waiting.pop(0))
            joined += 1
        for s in list(active):
            if next(s, _DONE) is _DONE:
                active.remove(s)
        rounds += 1


def _gla_operands(q, k, la, reverse, r0, qe_ref, ke_ref, kd_ref, dl_ref):
    R, C = la.shape[0], GLA_CHUNK
    rowc = jnp.bitwise_and(lax.broadcasted_iota(jnp.int32, la.shape, 0), C - 1)
    b = la
    d = 1
    while d < C:
        if reverse:
            valid, shift = rowc < C - d, R - d
        else:
            valid, shift = rowc >= d, d
        b = b + jnp.where(valid, pltpu.roll(b, shift, 0), 0.0)
        d *= 2
    for c in range(R // C):
        lo = c * C
        bc = b[lo:lo + C, :]
        bl = bc[0:1, :] if reverse else bc[C - 1:C, :]
        qe_ref[r0 + lo:r0 + lo + C, :] = (q[lo:lo + C, :] * jnp.exp(bc)).astype(BF16)
        ke_ref[r0 + lo:r0 + lo + C, :] = (k[lo:lo + C, :] * jnp.exp(-bc)).astype(BF16)
        kd_ref[r0 + lo:r0 + lo + C, :] = (k[lo:lo + C, :] * jnp.exp(bl - bc)).astype(BF16)
        ci = (r0 + lo) // C
        dl_ref[ci:ci + 1, :] = jnp.exp(bl)


def _inproj_pieces(r0, x_ref, xp_ref, xn_ref, gain_ref, w_ref, wlr_ref, wg_ref, bg_ref,
                   cw_ref, cb_ref, xc_ref, gate_ref, v_ref, g_ref,
                   qef_ref, kef_ref, kdf_ref, dlf_ref, qeb_ref, keb_ref, kdb_ref, dlb_ref):
    T, R = x_ref.shape[0], PROJ_ROWS
    rows = slice(r0, r0 + R)
    i, n = pl.program_id(0), pl.num_programs(0)
    if r0 == 0:
        x_prev, pad_prev = xp_ref[...], i == 0
    else:
        x_prev, pad_prev = x_ref[r0 - SUBLANES:r0, :], False
    if r0 + R == T:
        x_next, pad_next = xn_ref[...], i == n - 1
    else:
        x_next, pad_next = x_ref[r0 + R:r0 + R + SUBLANES, :], False
    x_ext = jnp.concatenate([x_ref[rows, :], x_prev, x_next], axis=0)
    h_ext = _rms_norm(x_ext, gain_ref[...]).astype(BF16)
    h = h_ext[:R, :]
    yield
    lr = _dot(h, wlr_ref[...]).astype(BF16)
    xr = _dot(h_ext, w_ref[:, _P_XR:_P_GR])
    yield
    z = _dot(lr, wg_ref[...]) + bg_ref[...]
    gate = _dot(h, w_ref[:, _P_GR:_P_Q])
    grouped = (R // SUBLANES, SUBLANES, xr.shape[1])
    sub = lax.broadcasted_iota(jnp.int32, grouped, 1)
    xg = xr[:R, :].reshape(grouped)
    prev8 = jnp.where(pad_prev, 0.0, xr[R:R + SUBLANES, :])[None]
    next8 = jnp.where(pad_next, 0.0, xr[R + SUBLANES:, :])[None]

    def shifted(k):
        rot = pltpu.roll(xg, (-k) % SUBLANES, 1)
        if k < 0:
            edge = pltpu.roll(prev8, (-k) % SUBLANES, 1)
            return jnp.where(sub >= -k, rot, jnp.concatenate([edge, rot[:-1]], axis=0))
        edge = pltpu.roll(next8, (-k) % SUBLANES, 1)
        return jnp.where(sub < SUBLANES - k, rot, jnp.concatenate([rot[1:], edge], axis=0))

    cw = cw_ref[...]
    xc = (cw[0:1, :] * shifted(-2) + cw[1:2, :] * shifted(-1) + cw[2:3, :] * xg
          + cw[3:4, :] * shifted(1) + cb_ref[...]).reshape(R, -1)
    for lo in range(0, R, LRU_SEG):
        _store_segment(xc_ref, r0 + lo, xc[lo:lo + LRU_SEG, :])
    yield
    q = _dot(h, w_ref[:, _P_Q:_P_K]) * (GLA_DK ** -0.5)
    k = _dot(h, w_ref[:, _P_K:_P_V])
    la = _log_sigmoid(z) * (1.0 / GLA_TAU)
    yield
    g = _dot(h, w_ref[:, _P_G:_P_END])
    gate_ref[rows, :] = _gelu_tanh(gate).astype(gate_ref.dtype)
    _gla_operands(q, k, la[:, :GLA_DK_TOTAL], False, r0, qef_ref, kef_ref, kdf_ref, dlf_ref)
    yield
    v_ref[rows, :] = _dot(h, w_ref[:, _P_V:_P_G]).astype(BF16)
    g_ref[rows, :] = (g * _sigmoid(g)).astype(g_ref.dtype)
    _gla_operands(q, k, la[:, GLA_DK_TOTAL:], True, r0, qeb_ref, keb_ref, kdb_ref, dlb_ref)


def _inproj_kernel(*refs):
    T = refs[0].shape[0]
    _round_robin((_inproj_pieces(r0, *refs) for r0 in range(0, T, PROJ_ROWS)),
                 stagger=PROJ_STAGGER)


def _inproj(x, layer, gain, w_main, w_lr, w_gate, b_gate, conv_w, conv_b):
    S = x.shape[0]
    T = TILE_PROJ
    halo_per_tile = T // HALO_ROWS
    n_halo = S // HALO_ROWS
    row = lambda n: pl.BlockSpec((T, n), lambda i: (i, 0))
    full = lambda a: _layer_spec(a, layer)
    prev = pl.BlockSpec((HALO_ROWS, D_MODEL),
                        lambda i: (jnp.maximum(i * halo_per_tile - 1, 0), 0))
    nxt = pl.BlockSpec((HALO_ROWS, D_MODEL),
                       lambda i: (jnp.minimum((i + 1) * halo_per_tile, n_halo - 1), 0))
    dl_spec = pl.BlockSpec((T // GLA_CHUNK, GLA_DK_TOTAL), lambda i: (i, 0))
    gla_shapes = (
        jax.ShapeDtypeStruct((S, GLA_DK_TOTAL), BF16),
        jax.ShapeDtypeStruct((S, GLA_DK_TOTAL), BF16),
        jax.ShapeDtypeStruct((S, GLA_DK_TOTAL), BF16),
        jax.ShapeDtypeStruct((S // GLA_CHUNK, GLA_DK_TOTAL), F32),
    )
    gla_specs = [row(GLA_DK_TOTAL)] * 3 + [dl_spec]
    out_shapes = (
        jax.ShapeDtypeStruct(_scan_layout_shape(S), F32),
        jax.ShapeDtypeStruct((S, D_RNN), ACT_DTYPE),
        jax.ShapeDtypeStruct((S, GLA_DV_TOTAL), BF16),
        jax.ShapeDtypeStruct((S, GLA_DV_TOTAL), ACT_DTYPE),
    ) + gla_shapes + gla_shapes
    params = (gain, w_main, w_lr, w_gate, b_gate, conv_w, conv_b)
    return pl.pallas_call(
        _inproj_kernel,
        grid=(S // T,),
        in_specs=[row(D_MODEL), prev, nxt] + [full(p) for p in params],
        out_specs=[_scan_layout_spec(T, lambda i: i), row(D_RNN), row(GLA_DV_TOTAL),
                   row(GLA_DV_TOTAL)] + gla_specs + gla_specs,
        out_shape=out_shapes,
        compiler_params=pltpu.CompilerParams(
            dimension_semantics=("parallel",), vmem_limit_bytes=VMEM_LIMIT_BYTES),
        name="inproj",
    )(x, x, x, *params)


def _lru_pieces(xc_ref, wa, ba, wx, bx, lam, reverse, carry_ref, out_ref, hl_scr, pl_scr):
    n_slabs = xc_ref.shape[1]
    T, width = xc_ref.shape[2], n_slabs * LANES
    R, Q = LRU_BLOCK, LRU_SUB
    n_blocks, n_sub, n_steps = T // R, R // Q, Q // SUBLANES
    sp = _softplus(-lam)
    k_exp2 = (-LRU_C * 1.4426950408889634) * sp
    k_tanh = LRU_C * sp
    backwards = lambda n: list(range(n - 1, -1, -1) if reverse else range(n))
    blocks = backwards(n_blocks)
    load = lambda lo, n: jnp.concatenate([xc_ref[0, s, lo:lo + n, :] for s in range(n_slabs)],
                                         axis=-1)

    def gate_matmuls(blk):
        xcb = load(blk * R, R).astype(BF16)
        return _dot(xcb, wa), _dot(xcb, wx)

    h = jnp.zeros((SUBLANES, width), F32)
    p = jnp.ones((SUBLANES, width), F32)
    pre = [gate_matmuls(blk) for blk in blocks[:LRU_AHEAD]]
    yield
    for n_done, blk in enumerate(blocks):
        ya, yx = pre.pop(0)
        for s_done, s in enumerate(backwards(n_sub)):
            if s_done == 0 and n_done + LRU_AHEAD < n_blocks:
                pre.append(gate_matmuls(blocks[n_done + LRU_AHEAD]))
            q0 = blk * R + s * Q
            xc = load(q0, Q)
            r = _sigmoid(ya[s * Q:(s + 1) * Q, :] + ba)
            gi = _sigmoid(yx[s * Q:(s + 1) * Q, :] + bx)
            a = jnp.exp2(r * k_exp2)
            z = (1.0 + a * a) * jnp.tanh(r * k_tanh)
            u = xc * gi * jnp.where(z > 0.0, z * lax.rsqrt(z), 0.0)
            hs, ps = [None] * n_steps, [None] * n_steps
            for g in backwards(n_steps):
                rows = slice(g * SUBLANES, (g + 1) * SUBLANES)
                h = a[rows, :] * h + u[rows, :]
                p = a[rows, :] * p
                hs[g], ps[g] = h, p
            hl_scr[q0:q0 + Q, :] = jnp.concatenate(hs, axis=0)
            pl_scr[q0:q0 + Q, :] = jnp.concatenate(ps, axis=0)
            yield

    sub = lax.broadcasted_iota(jnp.int32, (SUBLANES, width), 0)
    A, H = p, h
    for d in (1, 2, 4):
        if reverse:
            valid, shift = sub < SUBLANES - d, SUBLANES - d
        else:
            valid, shift = sub >= d, d
        A_s = jnp.where(valid, pltpu.roll(A, shift, 0), 1.0)
        H_s = jnp.where(valid, pltpu.roll(H, shift, 0), 0.0)
        H = H + A * H_s
        A = A * A_s
    c_tile = carry_ref[...]
    c_end = H + A * c_tile
    if reverse:
        c_in = jnp.where(sub == SUBLANES - 1, c_tile, pltpu.roll(c_end, SUBLANES - 1, 0))
        carry_ref[...] = c_end[0:1, :]
    else:
        c_in = jnp.where(sub == 0, c_tile, pltpu.roll(c_end, 1, 0))
        carry_ref[...] = c_end[SUBLANES - 1:SUBLANES, :]
    yield
    for blk in blocks:
        lo = blk * R
        grouped = (R // SUBLANES, SUBLANES, width)
        out = (hl_scr[lo:lo + R, :].reshape(grouped)
               + pl_scr[lo:lo + R, :].reshape(grouped) * c_in[None]).reshape(R, width)
        for s in range(n_slabs):
            out_ref[0, s, lo:lo + R, :] = out[:, s * LANES:(s + 1) * LANES]
        yield


def _gla_pieces(qe_ref, ke_ref, kd_ref, dl_ref, v_ref, st_ref, out_ref, reverse):
    T = qe_ref.shape[0]
    C = GLA_CHUNK
    lane_k = lax.broadcasted_iota(jnp.int32, (C, GLA_DK_TOTAL), 1)
    lane_s = lax.broadcasted_iota(jnp.int32, (GLA_DV, GLA_DK_TOTAL), 1)
    ri = lax.broadcasted_iota(jnp.int32, (C, C), 0)
    ci = lax.broadcasted_iota(jnp.int32, (C, C), 1)
    causal = (ci >= ri) if reverse else (ci <= ri)

    n_chunks = T // C
    order = list(range(n_chunks - 1, -1, -1) if reverse else range(n_chunks))
    rows = [slice(c * C, (c + 1) * C) for c in range(n_chunks)]

    def increment(c):
        return lax.dot_general(v_ref[rows[c], :], kd_ref[rows[c], :], (((0,), (0,)), ((), ())),
                               preferred_element_type=F32)

    def diagonal(p):
        u = p[:GLA_DV, :]
        for h in range(1, GLA_HEADS):
            u = jnp.where(lane_s >= h * GLA_DK, p[h * GLA_DV:(h + 1) * GLA_DV, :], u)
        return u

    def inter_and_scores(c, st_in):
        rhs = jnp.concatenate([st_in, ke_ref[rows[c], :]], axis=0)
        qe = qe_ref[rows[c], :]
        lhs = jnp.concatenate(
            [jnp.where((lane_k >= h * GLA_DK) & (lane_k < (h + 1) * GLA_DK), qe,
                       jnp.zeros_like(qe)) for h in range(GLA_HEADS)], axis=0)
        return lax.dot_general(lhs, rhs, (((1,), (1,)), ((), ())),
                               preferred_element_type=F32)

    def intra(c, so):
        parts = []
        for h in range(GLA_HEADS):
            soh = so[h * C:(h + 1) * C, :]
            scores = jnp.where(causal, soh[:, GLA_DV:], 0.0).astype(BF16)
            parts.append((soh[:, :GLA_DV],
                          _dot(scores, v_ref[rows[c], h * GLA_DV:(h + 1) * GLA_DV])))
        return parts

    def store(c, parts):
        for h, (o_inter, o_intra) in enumerate(parts):
            out_ref[rows[c], h * GLA_DV:(h + 1) * GLA_DV] = (o_inter + o_intra).astype(out_ref.dtype)

    st = st_ref[...]
    p, st_in, so, parts = {}, {}, {}, {}
    d1, d2, d3, d4 = GLA_STAGE_DELAYS
    for t in range(n_chunks + d4):
        if t < n_chunks:
            p[t] = increment(order[t])
        if 0 <= t - d1 < n_chunks:
            c = order[t - d1]
            st_in[t - d1] = st.astype(BF16)
            st = st * dl_ref[c:c + 1, :] + diagonal(p.pop(t - d1))
        if 0 <= t - d2 < n_chunks:
            so[t - d2] = inter_and_scores(order[t - d2], st_in.pop(t - d2))
        if 0 <= t - d3 < n_chunks:
            parts[t - d3] = intra(order[t - d3], so.pop(t - d3))
        if 0 <= t - d4 < n_chunks:
            store(order[t - d4], parts.pop(t - d4))
        yield
    st_ref[...] = st


def _mixer_kernel(xcf_ref, vf_ref, qef_ref, kef_ref, kdf_ref, dlf_ref,
                  xcb_ref, vb_ref, qeb_ref, keb_ref, kdb_ref, dlb_ref,
                  wa_ref, ba_ref, wx_ref, bx_ref, lam_ref,
                  hf_ref, hb_ref, of_ref, ob_ref,
                  cf_ref, cbk_ref, stf_ref, stb_ref, hlf_scr, plf_scr, hlb_scr, plb_scr):
    @pl.when(pl.program_id(0) == 0)
    def _():
        cf_ref[...] = jnp.zeros_like(cf_ref)
        cbk_ref[...] = jnp.zeros_like(cbk_ref)
        stf_ref[...] = jnp.zeros_like(stf_ref)
        stb_ref[...] = jnp.zeros_like(stb_ref)

    streams = [
        _gla_pieces(qef_ref, kef_ref, kdf_ref, dlf_ref, vf_ref, stf_ref, of_ref, False),
        _lru_pieces(xcf_ref, wa_ref[0], ba_ref[0:1, :], wx_ref[0], bx_ref[0:1, :],
                    lam_ref[0:1, :], False, cf_ref, hf_ref, hlf_scr, plf_scr),
        _gla_pieces(qeb_ref, keb_ref, kdb_ref, dlb_ref, vb_ref, stb_ref, ob_ref, True),
        _lru_pieces(xcb_ref, wa_ref[1], ba_ref[1:2, :], wx_ref[1], bx_ref[1:2, :],
                    lam_ref[1:2, :], True, cbk_ref, hb_ref, hlb_scr, plb_scr),
    ]
    _round_robin(streams, stagger=MIX_STAGGER)


def _mixer(xc, v, gla_f, gla_b, layer, wa, ba, wx, bx, lam):
    S = v.shape[0]
    T = TILE_MIX
    assert T == LRU_TILE
    n = S // T
    full = lambda a: _layer_spec(a, layer)

    def specs(rev):
        t = (lambda i: n - 1 - i) if rev else (lambda i: i)
        blk = lambda width: pl.BlockSpec((T, width), lambda i: (t(i), 0))
        dl = pl.BlockSpec((T // GLA_CHUNK, GLA_DK_TOTAL), lambda i: (t(i), 0))
        scan = _scan_layout_spec(T, t)
        ins = [scan, blk(GLA_DV_TOTAL), blk(GLA_DK_TOTAL), blk(GLA_DK_TOTAL),
               blk(GLA_DK_TOTAL), dl]
        return ins, scan, blk(GLA_DV_TOTAL)

    ins_f, h_f, o_f = specs(False)
    ins_b, h_b, o_b = specs(True)
    params = (wa, ba, wx, bx, lam)
    return pl.pallas_call(
        _mixer_kernel,
        grid=(n,),
        in_specs=ins_f + ins_b + [full(p) for p in params],
        out_specs=[h_f, h_b, o_f, o_b],
        out_shape=(jax.ShapeDtypeStruct(_scan_layout_shape(S), F32),) * 2
                  + (jax.ShapeDtypeStruct((S, GLA_DV_TOTAL), ACT_DTYPE),) * 2,
        scratch_shapes=[pltpu.VMEM((1, D_RNN), F32)] * 2
                       + [pltpu.VMEM((GLA_DV, GLA_DK_TOTAL), F32)] * 2
                       + [pltpu.VMEM((T, D_RNN), F32)] * 4,
        compiler_params=pltpu.CompilerParams(
            dimension_semantics=("arbitrary",), vmem_limit_bytes=VMEM_LIMIT_BYTES),
        name="mixer",
    )(xc, v, *gla_f, xc, v, *gla_b, *params)


def _out_ffn_pieces(r0, x_ref, hf_ref, hb_ref, gate_ref, of_ref, ob_ref, g_ref,
                    rnn_norm_ref, gla_norm_ref, wo_ref, mix_post_ref,
                    ffn_pre_ref, wg_ref, wu_ref, wd_ref, ffn_post_ref, y_ref):
    rows = slice(r0, r0 + OUT_ROWS)
    gla_gain = gla_norm_ref[...]
    f32 = lambda ref, cols=slice(None): ref[rows, cols].astype(F32)
    y_rnn = jnp.concatenate(
        [_rms_norm((_load_segment(hf_ref, r) + _load_segment(hb_ref, r))
                   * gate_ref[r:r + LRU_SEG, :].astype(F32), rnn_norm_ref[...]).astype(BF16)
         for r in range(r0, r0 + OUT_ROWS, LRU_SEG)], axis=0)
    parts = [y_rnn]
    for h in range(GLA_HEADS):
        cols = slice(h * GLA_DV, (h + 1) * GLA_DV)
        parts.append((_rms_norm(f32(of_ref, cols) + f32(ob_ref, cols), gla_gain)
                      * f32(g_ref, cols)).astype(BF16))
    y = jnp.concatenate(parts, axis=-1)
    m = _dot(y, wo_ref[...])
    yield
    x1 = x_ref[rows, :] + _rms_norm(m, mix_post_ref[...])
    h2 = _rms_norm(x1, ffn_pre_ref[...]).astype(BF16)
    d_ff = wg_ref.shape[1]
    cols = [slice(c, min(c + FFN_CHUNK, d_ff)) for c in range(0, d_ff, FFN_CHUNK)]
    n_chunks = len(cols)
    d1, d2 = FFN_STAGE_DELAYS
    au, z = {}, {}
    f = None
    for t in range(n_chunks + d2):
        if t < n_chunks:
            au[t] = (_dot(h2, wg_ref[:, cols[t]]), _dot(h2, wu_ref[:, cols[t]]))
        if 0 <= t - d1 < n_chunks:
            a, u = au.pop(t - d1)
            z[t - d1] = (a * _sigmoid(a) * u).astype(BF16)
        if 0 <= t - d2 < n_chunks:
            fj = _dot(z.pop(t - d2), wd_ref[cols[t - d2], :])
            f = fj if f is None else f + fj
        yield
    y_ref[rows, :] = x1 + _rms_norm(f, ffn_post_ref[...])


def _out_ffn_kernel(*refs):
    T = refs[0].shape[0]
    streams = [_out_ffn_pieces(r, *refs) for r in range(0, T, OUT_ROWS)]
    _round_robin(streams, stagger=OUT_STAGGER)


def _out_ffn(x, hf, hb, gate, of, ob, g, layer, rnn_norm, gla_norm, wo, mix_post,
             ffn_pre, wg, wu, wd, ffn_post):
    S = x.shape[0]
    T = TILE_OUT
    row = lambda n: pl.BlockSpec((T, n), lambda i: (i, 0))
    const = lambda a: _layer_spec(a, layer, pipeline_mode=pl.Buffered(1))
    return pl.pallas_call(
        _out_ffn_kernel,
        grid=(S // T,),
        in_specs=[row(D_MODEL), _scan_layout_spec(T, lambda i: i),
                  _scan_layout_spec(T, lambda i: i), row(D_RNN), row(GLA_DV_TOTAL),
                  row(GLA_DV_TOTAL), row(GLA_DV_TOTAL), const(rnn_norm), const(gla_norm),
                  const(wo), const(mix_post), const(ffn_pre),
                  const(wg), const(wu), const(wd), const(ffn_post)],
        out_specs=row(D_MODEL),
        out_shape=jax.ShapeDtypeStruct((S, D_MODEL), F32),
        compiler_params=pltpu.CompilerParams(
            dimension_semantics=("parallel",), vmem_limit_bytes=VMEM_LIMIT_BYTES),
        name="out_ffn",
    )(x, hf, hb, gate, of, ob, g, rnn_norm, gla_norm, wo, mix_post,
      ffn_pre, wg, wu, wd, ffn_post)


def _split_w_in_kernel(w_ref, main_ref, lr_ref):
    main_ref[...] = w_ref[:, :_P_END].astype(BF16)
    lr_ref[...] = jnp.zeros_like(lr_ref)
    lr_ref[:, :2 * GLA_RANK] = w_ref[:, _P_END:].astype(BF16)


def _split_w_in(w_in):
    L, D, d_in = w_in.shape
    rows = W_PREP_ROWS
    blk = lambda n: pl.BlockSpec((None, rows, n), lambda l, r: (l, r, 0))
    return pl.pallas_call(
        _split_w_in_kernel,
        grid=(L, D // rows),
        in_specs=[blk(d_in)],
        out_specs=[blk(_P_END), blk(LANES)],
        out_shape=(jax.ShapeDtypeStruct((L, D, _P_END), BF16),
                   jax.ShapeDtypeStruct((L, D, LANES), BF16)),
        compiler_params=pltpu.CompilerParams(dimension_semantics=("parallel", "parallel")),
        name="split_w_in",
    )(w_in)
def _block_diag(w):
    eye = jnp.eye(RNN_HEADS, dtype=w.dtype)
    dense = jnp.einsum('ldhij,hg->ldhigj', w, eye)
    return dense.reshape(w.shape[0], w.shape[1], D_RNN, D_RNN).astype(BF16)


def _rows(v):
    return v.reshape(v.shape[0], 1, -1)


def kernel(x, mix_norm_pre, mix_norm_post, w_in, conv_w, conv_b, lru_w_a, lru_b_a, lru_w_x,
           lru_b_x, lru_lambda, rnn_out_norm, gla_w_gate, gla_b_gate, gla_out_norm, w_out,
           ffn_norm_pre, ffn_norm_post, w_ffn_gate, w_ffn_up, w_ffn_down):
    B, S, D = x.shape
    depth = w_in.shape[0]
    assert B == 1
    xs = x.reshape(B * S, D)

    w_main, w_lr = _split_w_in(w_in)
    w_gate = jnp.zeros((depth, LANES, 2 * GLA_DK_TOTAL), F32)
    w_gate = w_gate.at[:, :GLA_RANK, :GLA_DK_TOTAL].set(gla_w_gate[:, 0])
    w_gate = w_gate.at[:, GLA_RANK:2 * GLA_RANK, GLA_DK_TOTAL:].set(gla_w_gate[:, 1])
    w_gate = w_gate.astype(BF16)
    b_gate = gla_b_gate.reshape(depth, 1, 2 * GLA_DK_TOTAL)
    wa, wx = _block_diag(lru_w_a), _block_diag(lru_w_x)
    wo, wg, wu, wd = (w.astype(BF16) for w in (w_out, w_ffn_gate, w_ffn_up, w_ffn_down))
    mix_pre, mix_post, ffn_pre, ffn_post, rnn_norm, gla_norm, cb = (
        _rows(v) for v in (mix_norm_pre, mix_norm_post, ffn_norm_pre, ffn_norm_post,
                           rnn_out_norm, gla_out_norm, conv_b))

    for l in range(depth):
        outs = _inproj(xs, l, mix_pre, w_main, w_lr, w_gate, b_gate, conv_w, cb)
        xc, gate, v, g = outs[:4]
        gla_f, gla_b = outs[4:8], outs[8:12]
        hf, hb, of, ob = _mixer(xc, v, gla_f, gla_b, l, wa, lru_b_a, wx, lru_b_x, lru_lambda)
        xs = _out_ffn(xs, hf, hb, gate, of, ob, g, l, rnn_norm, gla_norm, wo, mix_post,
                      ffn_pre, wg, wu, wd, ffn_post)
    return xs.reshape(B, S, D)
```

```python
import functools

import jax
import jax.numpy as jnp
from jax import lax
from jax.experimental import pallas as pl
from jax.experimental.pallas import tpu as pltpu

F32 = jnp.float32
BF16 = jnp.bfloat16
ACT_DTYPE = F32

D_MODEL = 1024
D_RNN = 512
RNN_HEADS = 8
RNN_BLOCK = D_RNN // RNN_HEADS
CONV_WIDTH = 4
LRU_C = 8.0
GLA_HEADS = 4
GLA_DV_TOTAL = 512
GLA_DK_TOTAL = 256
GLA_DV = GLA_DV_TOTAL // GLA_HEADS
GLA_DK = GLA_DK_TOTAL // GLA_HEADS
GLA_RANK = 16
GLA_TAU = 16.0
GLA_CHUNK = 64
RMS_EPS = 1e-6

SUBLANES = 8
LANES = 128
HALO_ROWS = SUBLANES
VMEM_LIMIT_BYTES = 56 * 1024 * 1024

TILE_PROJ = 1024
PROJ_ROWS = 128
PROJ_STAGGER = 0
TILE_MIX = 512
TILE_OUT = 512
OUT_ROWS = 256
FFN_CHUNK = 256
FFN_STAGE_DELAYS = (1, 1)
OUT_STAGGER = 0
GATE_TILE = 256
LRU_TILE = 512
LRU_SEG = LRU_TILE // SUBLANES
LRU_BLOCK = 128
LRU_AHEAD = 1
LRU_SUB = 64
GLA_STAGE_DELAYS = (2, 4, 6, 8)
W_PREP_ROWS = 256
MIX_STAGGER = 0
_DONE = object()

_P_XR = 0
_P_GR = _P_XR + D_RNN
_P_Q = _P_GR + D_RNN
_P_K = _P_Q + GLA_DK_TOTAL
_P_V = _P_K + GLA_DK_TOTAL
_P_G = _P_V + GLA_DV_TOTAL
_P_END = _P_G + GLA_DV_TOTAL


def _rms_norm(x, gain):
    ms = jnp.mean(x * x, axis=-1, keepdims=True)
    return x * lax.rsqrt(ms + RMS_EPS) * gain


def _sigmoid(x):
    return 1.0 / (1.0 + jnp.exp(-x))


def _softplus(x):
    return jnp.maximum(x, 0.0) + jnp.log1p(jnp.exp(-jnp.abs(x)))


def _log_sigmoid(x):
    return jnp.minimum(x, 0.0) - jnp.log(1.0 + jnp.exp(-jnp.abs(x)))


def _gelu_tanh(x):
    c = 0.7978845608028654
    half = 0.5 * x
    return half + half * jnp.tanh(x * (c + (c * 0.044715) * (x * x)))


def _dot(a, b):
    return jnp.dot(a, b, preferred_element_type=F32)


def _layer_spec(a, layer, **kwargs):
    return pl.BlockSpec((None,) + a.shape[1:], lambda i: (layer,) + (0,) * (a.ndim - 1), **kwargs)


def _scan_layout_shape(S):
    return (S // LRU_TILE, D_RNN // LANES, LRU_TILE, LANES)


def _scan_layout_spec(tile_rows, index):
    return pl.BlockSpec((tile_rows // LRU_TILE, D_RNN // LANES, LRU_TILE, LANES),
                        lambda i: (index(i), 0, 0, 0))


def _store_segment(ref, row0, vals):
    tile, seg = row0 // LRU_TILE, (row0 % LRU_TILE) // LRU_SEG
    for s in range(D_RNN // LANES):
        ref[tile, s, pl.ds(seg, LRU_SEG, stride=SUBLANES), :] = vals[:, s * LANES:(s + 1) * LANES]


def _load_segment(ref, row0):
    tile, seg = row0 // LRU_TILE, (row0 % LRU_TILE) // LRU_SEG
    return jnp.concatenate([ref[tile, s, pl.ds(seg, LRU_SEG, stride=SUBLANES), :]
                            for s in range(D_RNN // LANES)], axis=-1)


def _round_robin(streams, stagger=0):
    waiting = list(streams)
    active = []
    joined = rounds = 0
    while waiting or active:
        while waiting and joined * stagger <= rounds:
            active.append(waiting.pop(0))
            joined += 1
        for s in list(active):
            if next(s, _DONE) is _DONE:
                active.remove(s)
        rounds += 1


def _gla_operands(q, k, la, reverse, r0, qe_ref, ke_ref, kd_ref, dl_ref):
    R, C = la.shape[0], GLA_CHUNK
    rowc = jnp.bitwise_and(lax.broadcasted_iota(jnp.int32, la.shape, 0), C - 1)
    b = la
    d = 1
    while d < C:
        if reverse:
            valid, shift = rowc < C - d, R - d
        else:
            valid, shift = rowc >= d, d
        b = b + jnp.where(valid, pltpu.roll(b, shift, 0), 0.0)
        d *= 2
    for c in range(R // C):
        lo = c * C
        bc = b[lo:lo + C, :]
        bl = bc[0:1, :] if reverse else bc[C - 1:C, :]
        qe_ref[r0 + lo:r0 + lo + C, :] = (q[lo:lo + C, :] * jnp.exp(bc)).astype(BF16)
        ke_ref[r0 + lo:r0 + lo + C, :] = (k[lo:lo + C, :] * jnp.exp(-bc)).astype(BF16)
        kd_ref[r0 + lo:r0 + lo + C, :] = (k[lo:lo + C, :] * jnp.exp(bl - bc)).astype(BF16)
        ci = (r0 + lo) // C
        dl_ref[ci:ci + 1, :] = jnp.exp(bl)


def _inproj_pieces(r0, x_ref, xp_ref, xn_ref, gain_ref, w_ref, wlr_ref, wg_ref, bg_ref,
                   cw_ref, cb_ref, xc_ref, gate_ref, v_ref, g_ref,
                   qef_ref, kef_ref, kdf_ref, dlf_ref, qeb_ref, keb_ref, kdb_ref, dlb_ref):
    T, R = x_ref.shape[0], PROJ_ROWS
    rows = slice(r0, r0 + R)
    i, n = pl.program_id(0), pl.num_programs(0)
    if r0 == 0:
        x_prev, pad_prev = xp_ref[...], i == 0
    else:
        x_prev, pad_prev = x_ref[r0 - SUBLANES:r0, :], False
    if r0 + R == T:
        x_next, pad_next = xn_ref[...], i == n - 1
    else:
        x_next, pad_next = x_ref[r0 + R:r0 + R + SUBLANES, :], False
    x_ext = jnp.concatenate([x_ref[rows, :], x_prev, x_next], axis=0)
    h_ext = _rms_norm(x_ext, gain_ref[...]).astype(BF16)
    h = h_ext[:R, :]
    yield
    lr = _dot(h, wlr_ref[...]).astype(BF16)
    xr = _dot(h_ext, w_ref[:, _P_XR:_P_GR])
    yield
    z = _dot(lr, wg_ref[...]) + bg_ref[...]
    gate = _dot(h, w_ref[:, _P_GR:_P_Q])
    grouped = (R // SUBLANES, SUBLANES, xr.shape[1])
    sub = lax.broadcasted_iota(jnp.int32, grouped, 1)
    xg = xr[:R, :].reshape(grouped)
    prev8 = jnp.where(pad_prev, 0.0, xr[R:R + SUBLANES, :])[None]
    next8 = jnp.where(pad_next, 0.0, xr[R + SUBLANES:, :])[None]

    def shifted(k):
        rot = pltpu.roll(xg, (-k) % SUBLANES, 1)
        if k < 0:
            edge = pltpu.roll(prev8, (-k) % SUBLANES, 1)
            return jnp.where(sub >= -k, rot, jnp.concatenate([edge, rot[:-1]], axis=0))
        edge = pltpu.roll(next8, (-k) % SUBLANES, 1)
        return jnp.where(sub < SUBLANES - k, rot, jnp.concatenate([rot[1:], edge], axis=0))

    cw = cw_ref[...]
    xc = (cw[0:1, :] * shifted(-2) + cw[1:2, :] * shifted(-1) + cw[2:3, :] * xg
          + cw[3:4, :] * shifted(1) + cb_ref[...]).reshape(R, -1)
    for lo in range(0, R, LRU_SEG):
        _store_segment(xc_ref, r0 + lo, xc[lo:lo + LRU_SEG, :])
    yield
    q = _dot(h, w_ref[:, _P_Q:_P_K]) * (GLA_DK ** -0.5)
    k = _dot(h, w_ref[:, _P_K:_P_V])
    la = _log_sigmoid(z) * (1.0 / GLA_TAU)
    yield
    g = _dot(h, w_ref[:, _P_G:_P_END])
    gate_ref[rows, :] = _gelu_tanh(gate).astype(gate_ref.dtype)
    _gla_operands(q, k, la[:, :GLA_DK_TOTAL], False, r0, qef_ref, kef_ref, kdf_ref, dlf_ref)
    yield
    v_ref[rows, :] = _dot(h, w_ref[:, _P_V:_P_G]).astype(BF16)
    g_ref[rows, :] = (g * _sigmoid(g)).astype(g_ref.dtype)
    _gla_operands(q, k, la[:, GLA_DK_TOTAL:], True, r0, qeb_ref, keb_ref, kdb_ref, dlb_ref)


def _inproj_kernel(*refs):
    T = refs[0].shape[0]
    _round_robin((_inproj_pieces(r0, *refs) for r0 in range(0, T, PROJ_ROWS)),
                 stagger=PROJ_STAGGER)


def _inproj(x, layer, gain, w_main, w_lr, w_gate, b_gate, conv_w, conv_b):
    S = x.shape[0]
    T = TILE_PROJ
    halo_per_tile = T // HALO_ROWS
    n_halo = S // HALO_ROWS
    row = lambda n: pl.BlockSpec((T, n), lambda i: (i, 0))
    full = lambda a: _layer_spec(a, layer)
    prev = pl.BlockSpec((HALO_ROWS, D_MODEL),
                        lambda i: (jnp.maximum(i * halo_per_tile - 1, 0), 0))
    nxt = pl.BlockSpec((HALO_ROWS, D_MODEL),
                       lambda i: (jnp.minimum((i + 1) * halo_per_tile, n_halo - 1), 0))
    dl_spec = pl.BlockSpec((T // GLA_CHUNK, GLA_DK_TOTAL), lambda i: (i, 0))
    gla_shapes = (
        jax.ShapeDtypeStruct((S, GLA_DK_TOTAL), BF16),
        jax.ShapeDtypeStruct((S, GLA_DK_TOTAL), BF16),
        jax.ShapeDtypeStruct((S, GLA_DK_TOTAL), BF16),
        jax.ShapeDtypeStruct((S // GLA_CHUNK, GLA_DK_TOTAL), F32),
    )
    gla_specs = [row(GLA_DK_TOTAL)] * 3 + [dl_spec]
    out_shapes = (
        jax.ShapeDtypeStruct(_scan_layout_shape(S), F32),
        jax.ShapeDtypeStruct((S, D_RNN), ACT_DTYPE),
        jax.ShapeDtypeStruct((S, GLA_DV_TOTAL), BF16),
        jax.ShapeDtypeStruct((S, GLA_DV_TOTAL), ACT_DTYPE),
    ) + gla_shapes + gla_shapes
    params = (gain, w_main, w_lr, w_gate, b_gate, conv_w, conv_b)
    return pl.pallas_call(
        _inproj_kernel,
        grid=(S // T,),
        in_specs=[row(D_MODEL), prev, nxt] + [full(p) for p in params],
        out_specs=[_scan_layout_spec(T, lambda i: i), row(D_RNN), row(GLA_DV_TOTAL),
                   row(GLA_DV_TOTAL)] + gla_specs + gla_specs,
        out_shape=out_shapes,
        compiler_params=pltpu.CompilerParams(
            dimension_semantics=("parallel",), vmem_limit_bytes=VMEM_LIMIT_BYTES),
        name="inproj",
    )(x, x, x, *params)


def _lru_pieces(xc_ref, wa, ba, wx, bx, lam, reverse, carry_ref, out_ref, hl_scr, pl_scr):
    n_slabs = xc_ref.shape[1]
    T, width = xc_ref.shape[2], n_slabs * LANES
    R, Q = LRU_BLOCK, LRU_SUB
    n_blocks, n_sub, n_steps = T // R, R // Q, Q // SUBLANES
    sp = _softplus(-lam)
    k_exp2 = (-LRU_C * 1.4426950408889634) * sp
    k_tanh = LRU_C * sp
    backwards = lambda n: list(range(n - 1, -1, -1) if reverse else range(n))
    blocks = backwards(n_blocks)
    load = lambda lo, n: jnp.concatenate([xc_ref[0, s, lo:lo + n, :] for s in range(n_slabs)],
                                         axis=-1)

    def gate_matmuls(blk):
        xcb = load(blk * R, R).astype(BF16)
        tiles = [slice(t * GATE_TILE, (t + 1) * GATE_TILE) for t in range(wa.shape[0])]
        gate = lambda w: jnp.concatenate([_dot(xcb[:, c], w[t]) for t, c in enumerate(tiles)],
                                         axis=-1)
        return gate(wa), gate(wx)

    h = jnp.zeros((SUBLANES, width), F32)
    p = jnp.ones((SUBLANES, width), F32)
    pre = [gate_matmuls(blk) for blk in blocks[:LRU_AHEAD]]
    yield
    for n_done, blk in enumerate(blocks):
        ya, yx = pre.pop(0)
        for s_done, s in enumerate(backwards(n_sub)):
            if s_done == 0 and n_done + LRU_AHEAD < n_blocks:
                pre.append(gate_matmuls(blocks[n_done + LRU_AHEAD]))
            q0 = blk * R + s * Q
            xc = load(q0, Q)
            r = _sigmoid(ya[s * Q:(s + 1) * Q, :] + ba)
            gi = _sigmoid(yx[s * Q:(s + 1) * Q, :] + bx)
            a = jnp.exp2(r * k_exp2)
            z = (1.0 + a * a) * jnp.tanh(r * k_tanh)
            u = xc * gi * jnp.where(z > 0.0, z * lax.rsqrt(z), 0.0)
            hs, ps = [None] * n_steps, [None] * n_steps
            for g in backwards(n_steps):
                rows = slice(g * SUBLANES, (g + 1) * SUBLANES)
                h = a[rows, :] * h + u[rows, :]
                p = a[rows, :] * p
                hs[g], ps[g] = h, p
            hl_scr[q0:q0 + Q, :] = jnp.concatenate(hs, axis=0)
            pl_scr[q0:q0 + Q, :] = jnp.concatenate(ps, axis=0)
            yield

    sub = lax.broadcasted_iota(jnp.int32, (SUBLANES, width), 0)
    A, H = p, h
    for d in (1, 2, 4):
        if reverse:
            valid, shift = sub < SUBLANES - d, SUBLANES - d
        else:
            valid, shift = sub >= d, d
        A_s = jnp.where(valid, pltpu.roll(A, shift, 0), 1.0)
        H_s = jnp.where(valid, pltpu.roll(H, shift, 0), 0.0)
        H = H + A * H_s
        A = A * A_s
    c_tile = carry_ref[...]
    c_end = H + A * c_tile
    if reverse:
        c_in = jnp.where(sub == SUBLANES - 1, c_tile, pltpu.roll(c_end, SUBLANES - 1, 0))
        carry_ref[...] = c_end[0:1, :]
    else:
        c_in = jnp.where(sub == 0, c_tile, pltpu.roll(c_end, 1, 0))
        carry_ref[...] = c_end[SUBLANES - 1:SUBLANES, :]
    yield
    for blk in blocks:
        lo = blk * R
        grouped = (R // SUBLANES, SUBLANES, width)
        out = (hl_scr[lo:lo + R, :].reshape(grouped)
               + pl_scr[lo:lo + R, :].reshape(grouped) * c_in[None]).reshape(R, width)
        for s in range(n_slabs):
            out_ref[0, s, lo:lo + R, :] = out[:, s * LANES:(s + 1) * LANES]
        yield


def _gla_pieces(qe_ref, ke_ref, kd_ref, dl_ref, v_ref, st_ref, out_ref, reverse):
    T = qe_ref.shape[0]
    C = GLA_CHUNK
    lane_k = lax.broadcasted_iota(jnp.int32, (C, GLA_DK_TOTAL), 1)
    lane_s = lax.broadcasted_iota(jnp.int32, (GLA_DV, GLA_DK_TOTAL), 1)
    ri = lax.broadcasted_iota(jnp.int32, (C, C), 0)
    ci = lax.broadcasted_iota(jnp.int32, (C, C), 1)
    causal = (ci >= ri) if reverse else (ci <= ri)

    n_chunks = T // C
    order = list(range(n_chunks - 1, -1, -1) if reverse else range(n_chunks))
    rows = [slice(c * C, (c + 1) * C) for c in range(n_chunks)]

    def increment(c):
        return lax.dot_general(v_ref[rows[c], :], kd_ref[rows[c], :], (((0,), (0,)), ((), ())),
                               preferred_element_type=F32)

    def diagonal(p):
        u = p[:GLA_DV, :]
        for h in range(1, GLA_HEADS):
            u = jnp.where(lane_s >= h * GLA_DK, p[h * GLA_DV:(h + 1) * GLA_DV, :], u)
        return u

    def inter_and_scores(c, st_in):
        rhs = jnp.concatenate([st_in, ke_ref[rows[c], :]], axis=0)
        qe = qe_ref[rows[c], :]
        lhs = jnp.concatenate(
            [jnp.where((lane_k >= h * GLA_DK) & (lane_k < (h + 1) * GLA_DK), qe,
                       jnp.zeros_like(qe)) for h in range(GLA_HEADS)], axis=0)
        return lax.dot_general(lhs, rhs, (((1,), (1,)), ((), ())),
                               preferred_element_type=F32)

    def intra(c, so):
        parts = []
        for h in range(GLA_HEADS):
            soh = so[h * C:(h + 1) * C, :]
            scores = jnp.where(causal, soh[:, GLA_DV:], 0.0).astype(BF16)
            parts.append((soh[:, :GLA_DV],
                          _dot(scores, v_ref[rows[c], h * GLA_DV:(h + 1) * GLA_DV])))
        return parts

    def store(c, parts):
        for h, (o_inter, o_intra) in enumerate(parts):
            out_ref[rows[c], h * GLA_DV:(h + 1) * GLA_DV] = (o_inter + o_intra).astype(out_ref.dtype)

    st = st_ref[...]
    p, st_in, so, parts = {}, {}, {}, {}
    d1, d2, d3, d4 = GLA_STAGE_DELAYS
    for t in range(n_chunks + d4):
        if t < n_chunks:
            p[t] = increment(order[t])
        if 0 <= t - d1 < n_chunks:
            c = order[t - d1]
            st_in[t - d1] = st.astype(BF16)
            st = st * dl_ref[c:c + 1, :] + diagonal(p.pop(t - d1))
        if 0 <= t - d2 < n_chunks:
            so[t - d2] = inter_and_scores(order[t - d2], st_in.pop(t - d2))
        if 0 <= t - d3 < n_chunks:
            parts[t - d3] = intra(order[t - d3], so.pop(t - d3))
        if 0 <= t - d4 < n_chunks:
            store(order[t - d4], parts.pop(t - d4))
        yield
    st_ref[...] = st


def _mixer_kernel(xcf_ref, vf_ref, qef_ref, kef_ref, kdf_ref, dlf_ref,
                  xcb_ref, vb_ref, qeb_ref, keb_ref, kdb_ref, dlb_ref,
                  wa_ref, ba_ref, wx_ref, bx_ref, lam_ref,
                  hf_ref, hb_ref, of_ref, ob_ref,
                  cf_ref, cbk_ref, stf_ref, stb_ref, hlf_scr, plf_scr, hlb_scr, plb_scr):
    @pl.when(pl.program_id(0) == 0)
    def _():
        cf_ref[...] = jnp.zeros_like(cf_ref)
        cbk_ref[...] = jnp.zeros_like(cbk_ref)
        stf_ref[...] = jnp.zeros_like(stf_ref)
        stb_ref[...] = jnp.zeros_like(stb_ref)

    streams = [
        _gla_pieces(qef_ref, kef_ref, kdf_ref, dlf_ref, vf_ref, stf_ref, of_ref, False),
        _lru_pieces(xcf_ref, wa_ref[0], ba_ref[0:1, :], wx_ref[0], bx_ref[0:1, :],
                    lam_ref[0:1, :], False, cf_ref, hf_ref, hlf_scr, plf_scr),
        _gla_pieces(qeb_ref, keb_ref, kdb_ref, dlb_ref, vb_ref, stb_ref, ob_ref, True),
        _lru_pieces(xcb_ref, wa_ref[1], ba_ref[1:2, :], wx_ref[1], bx_ref[1:2, :],
                    lam_ref[1:2, :], True, cbk_ref, hb_ref, hlb_scr, plb_scr),
    ]
    _round_robin(streams, stagger=MIX_STAGGER)


def _mixer(xc, v, gla_f, gla_b, layer, wa, ba, wx, bx, lam):
    S = v.shape[0]
    T = TILE_MIX
    assert T == LRU_TILE
    n = S // T
    full = lambda a: _layer_spec(a, layer)

    def specs(rev):
        t = (lambda i: n - 1 - i) if rev else (lambda i: i)
        blk = lambda width: pl.BlockSpec((T, width), lambda i: (t(i), 0))
        dl = pl.BlockSpec((T // GLA_CHUNK, GLA_DK_TOTAL), lambda i: (t(i), 0))
        scan = _scan_layout_spec(T, t)
        ins = [scan, blk(GLA_DV_TOTAL), blk(GLA_DK_TOTAL), blk(GLA_DK_TOTAL),
               blk(GLA_DK_TOTAL), dl]
        return ins, scan, blk(GLA_DV_TOTAL)

    ins_f, h_f, o_f = specs(False)
    ins_b, h_b, o_b = specs(True)
    params = (wa, ba, wx, bx, lam)
    return pl.pallas_call(
        _mixer_kernel,
        grid=(n,),
        in_specs=ins_f + ins_b + [full(p) for p in params],
        out_specs=[h_f, h_b, o_f, o_b],
        out_shape=(jax.ShapeDtypeStruct(_scan_layout_shape(S), F32),) * 2
                  + (jax.ShapeDtypeStruct((S, GLA_DV_TOTAL), ACT_DTYPE),) * 2,
        scratch_shapes=[pltpu.VMEM((1, D_RNN), F32)] * 2
                       + [pltpu.VMEM((GLA_DV, GLA_DK_TOTAL), F32)] * 2
                       + [pltpu.VMEM((T, D_RNN), F32)] * 4,
        compiler_params=pltpu.CompilerParams(
            dimension_semantics=("arbitrary",), vmem_limit_bytes=VMEM_LIMIT_BYTES),
        name="mixer",
    )(xc, v, *gla_f, xc, v, *gla_b, *params)


def _out_ffn_pieces(r0, x_ref, hf_ref, hb_ref, gate_ref, of_ref, ob_ref, g_ref,
                    rnn_norm_ref, gla_norm_ref, wo_ref, mix_post_ref,
                    ffn_pre_ref, wg_ref, wu_ref, wd_ref, ffn_post_ref, y_ref):
    rows = slice(r0, r0 + OUT_ROWS)
    gla_gain = gla_norm_ref[...]
    f32 = lambda ref, cols=slice(None): ref[rows, cols].astype(F32)
    y_rnn = jnp.concatenate(
        [_rms_norm((_load_segment(hf_ref, r) + _load_segment(hb_ref, r))
                   * gate_ref[r:r + LRU_SEG, :].astype(F32), rnn_norm_ref[...]).astype(BF16)
         for r in range(r0, r0 + OUT_ROWS, LRU_SEG)], axis=0)
    parts = [y_rnn]
    for h in range(GLA_HEADS):
        cols = slice(h * GLA_DV, (h + 1) * GLA_DV)
        parts.append((_rms_norm(f32(of_ref, cols) + f32(ob_ref, cols), gla_gain)
                      * f32(g_ref, cols)).astype(BF16))
    y = jnp.concatenate(parts, axis=-1)
    m = _dot(y, wo_ref[...])
    yield
    x1 = x_ref[rows, :] + _rms_norm(m, mix_post_ref[...])
    h2 = _rms_norm(x1, ffn_pre_ref[...]).astype(BF16)
    d_ff = wg_ref.shape[1]
    cols = [slice(c, min(c + FFN_CHUNK, d_ff)) for c in range(0, d_ff, FFN_CHUNK)]
    n_chunks = len(cols)
    d1, d2 = FFN_STAGE_DELAYS
    au, z = {}, {}
    f = None
    for t in range(n_chunks + d2):
        if t < n_chunks:
            au[t] = (_dot(h2, wg_ref[:, cols[t]]), _dot(h2, wu_ref[:, cols[t]]))
        if 0 <= t - d1 < n_chunks:
            a, u = au.pop(t - d1)
            z[t - d1] = (a * _sigmoid(a) * u).astype(BF16)
        if 0 <= t - d2 < n_chunks:
            fj = _dot(z.pop(t - d2), wd_ref[cols[t - d2], :])
            f = fj if f is None else f + fj
        yield
    y_ref[rows, :] = x1 + _rms_norm(f, ffn_post_ref[...])


def _out_ffn_kernel(*refs):
    T = refs[0].shape[0]
    streams = [_out_ffn_pieces(r, *refs) for r in range(0, T, OUT_ROWS)]
    _round_robin(streams, stagger=OUT_STAGGER)


def _out_ffn(x, hf, hb, gate, of, ob, g, layer, rnn_norm, gla_norm, wo, mix_post,
             ffn_pre, wg, wu, wd, ffn_post):
    S = x.shape[0]
    T = TILE_OUT
    row = lambda n: pl.BlockSpec((T, n), lambda i: (i, 0))
    const = lambda a: _layer_spec(a, layer, pipeline_mode=pl.Buffered(1))
    return pl.pallas_call(
        _out_ffn_kernel,
        grid=(S // T,),
        in_specs=[row(D_MODEL), _scan_layout_spec(T, lambda i: i),
                  _scan_layout_spec(T, lambda i: i), row(D_RNN), row(GLA_DV_TOTAL),
                  row(GLA_DV_TOTAL), row(GLA_DV_TOTAL), const(rnn_norm), const(gla_norm),
                  const(wo), const(mix_post), const(ffn_pre),
                  const(wg), const(wu), const(wd), const(ffn_post)],
        out_specs=row(D_MODEL),
        out_shape=jax.ShapeDtypeStruct((S, D_MODEL), F32),
        compiler_params=pltpu.CompilerParams(
            dimension_semantics=("parallel",), vmem_limit_bytes=VMEM_LIMIT_BYTES),
        name="out_ffn",
    )(x, hf, hb, gate, of, ob, g, rnn_norm, gla_norm, wo, mix_post,
      ffn_pre, wg, wu, wd, ffn_post)


def _split_w_in_kernel(w_ref, main_ref, lr_ref):
    main_ref[...] = w_ref[:, :_P_END].astype(BF16)
    lr_ref[...] = jnp.zeros_like(lr_ref)
    lr_ref[:, :2 * GLA_RANK] = w_ref[:, _P_END:].astype(BF16)


def _split_w_in(w_in):
    L, D, d_in = w_in.shape
    rows = W_PREP_ROWS
    blk = lambda n: pl.BlockSpec((None, rows, n), lambda l, r: (l, r, 0))
    return pl.pallas_call(
        _split_w_in_kernel,
        grid=(L, D // rows),
        in_specs=[blk(d_in)],
        out_specs=[blk(_P_END), blk(LANES)],
        out_shape=(jax.ShapeDtypeStruct((L, D, _P_END), BF16),
                   jax.ShapeDtypeStruct((L, D, LANES), BF16)),
        compiler_params=pltpu.CompilerParams(dimension_semantics=("parallel", "parallel")),
        name="split_w_in",
    )(w_in)


def _block_diag(w):
    L, n_dir = w.shape[:2]
    per_tile = GATE_TILE // RNN_BLOCK
    n_tiles = RNN_HEADS // per_tile
    eye = jnp.eye(per_tile, dtype=w.dtype)
    w = w.reshape(L, n_dir, n_tiles, per_tile, RNN_BLOCK, RNN_BLOCK)
    dense = jnp.einsum('ldthij,hg->ldthigj', w, eye)
    return dense.reshape(L, n_dir, n_tiles, GATE_TILE, GATE_TILE).astype(BF16)


def _rows(v):
    return v.reshape(v.shape[0], 1, -1)


def kernel(x, mix_norm_pre, mix_norm_post, w_in, conv_w, conv_b, lru_w_a, lru_b_a, lru_w_x,
           lru_b_x, lru_lambda, rnn_out_norm, gla_w_gate, gla_b_gate, gla_out_norm, w_out,
           ffn_norm_pre, ffn_norm_post, w_ffn_gate, w_ffn_up, w_ffn_down):
    B, S, D = x.shape
    depth = w_in.shape[0]
    assert B == 1
    xs = x.reshape(B * S, D)

    w_main, w_lr = _split_w_in(w_in)
    w_gate = jnp.zeros((depth, LANES, 2 * GLA_DK_TOTAL), F32)
    w_gate = w_gate.at[:, :GLA_RANK, :GLA_DK_TOTAL].set(gla_w_gate[:, 0])
    w_gate = w_gate.at[:, GLA_RANK:2 * GLA_RANK, GLA_DK_TOTAL:].set(gla_w_gate[:, 1])
    w_gate = w_gate.astype(BF16)
    b_gate = gla_b_gate.reshape(depth, 1, 2 * GLA_DK_TOTAL)
    wa, wx = _block_diag(lru_w_a), _block_diag(lru_w_x)
    wo, wg, wu, wd = (w.astype(BF16) for w in (w_out, w_ffn_gate, w_ffn_up, w_ffn_down))
    mix_pre, mix_post, ffn_pre, ffn_post, rnn_norm, gla_norm, cb = (
        _rows(v) for v in (mix_norm_pre, mix_norm_post, ffn_norm_pre, ffn_norm_post,
                           rnn_out_norm, gla_out_norm, conv_b))

    for l in range(depth):
        outs = _inproj(xs, l, mix_pre, w_main, w_lr, w_gate, b_gate, conv_w, cb)
        xc, gate, v, g = outs[:4]
        gla_f, gla_b = outs[4:8], outs[8:12]
        hf, hb, of, ob = _mixer(xc, v, gla_f, gla_b, l, wa, lru_b_a, wx, lru_b_x, lru_lambda)
        xs = _out_ffn(xs, hf, hb, gate, of, ob, g, l, rnn_norm, gla_norm, wo, mix_post,
                      ffn_pre, wg, wu, wd, ffn_post)
    return xs.reshape(B, S, D)
```
